```python
import numpy as np
import jax
import jax.numpy as jnp
from jax import lax

D_MODEL = 4096
BATCH = 16
SEQ = 256
DEPTH = 2
DEC_BATCH = 4
DEC_SEQ = 2048
PAST_LEN = 256

GRID_W = 64
HEAD_DIM = 128
N_BRANCH = 4
BRANCH_WIDTH = D_MODEL // 4
N_NA_HEADS = BRANCH_WIDTH // HEAD_DIM
NA_WIN_R = 8
NA_WIN_C = 16
N_GQA_HEADS = BRANCH_WIDTH // HEAD_DIM
N_KV_HEADS = 2
GQA_GROUP = N_GQA_HEADS // N_KV_HEADS
ROPE_THETA = 10000.0
ROPE_AXIS_PAIRS = HEAD_DIM // 4
CONV_WIDTH = BRANCH_WIDTH
CONV_K = 3
SGU_WIDTH = BRANCH_WIDTH
SGU_GROUPS = 8
CHUNK = 128
N_EXPERTS = 32
TOP_K = 4
EXPERT_FF = D_MODEL // 2
SWIGLU_ALPHA = 1.702
SWIGLU_LIMIT = 7.0
MOE_BLOCK = 256
Q_BLOCK = 128
NORM_EPS = 1e-6
NEG_INF = -1e30
IN_WIDTHS = (BRANCH_WIDTH, BRANCH_WIDTH, BRANCH_WIDTH,
             N_GQA_HEADS * HEAD_DIM, N_KV_HEADS * HEAD_DIM, N_KV_HEADS * HEAD_DIM,
             CONV_WIDTH, CONV_WIDTH, CONV_WIDTH,
             SGU_WIDTH, SGU_WIDTH)
IN_WIDTH = 3 * BRANCH_WIDTH + (N_GQA_HEADS + 2 * N_KV_HEADS) * HEAD_DIM + 3 * CONV_WIDTH + 2 * SGU_WIDTH

kernel_name = 'hybrid_diffusion_trunk_step'


def rmsnorm(x):
    xf = x.astype(jnp.float32)
    return (xf * lax.rsqrt(jnp.mean(xf * xf, axis=-1, keepdims=True) + NORM_EPS)).astype(x.dtype)


def layernorm(x, g, b):
    xf = x.astype(jnp.float32)
    mu = jnp.mean(xf, axis=-1, keepdims=True)
    var = jnp.mean(jnp.square(xf - mu), axis=-1, keepdims=True)
    return ((xf - mu) * lax.rsqrt(var + NORM_EPS)).astype(x.dtype) * g + b


def axial_rope_tables(length):
    t = jnp.arange(length)
    row = (t // GRID_W).astype(jnp.float32)
    col = (t % GRID_W).astype(jnp.float32)
    inv = ROPE_THETA ** (-jnp.arange(ROPE_AXIS_PAIRS, dtype=jnp.float32) / ROPE_AXIS_PAIRS)
    ang = jnp.concatenate([row[:, None] * inv, col[:, None] * inv], axis=-1)
    return jnp.cos(ang), jnp.sin(ang)


def apply_rope(x, cos, sin):
    x1, x2 = jnp.split(x.astype(jnp.float32), 2, axis=-1)
    c = cos[:, None, :]
    s = sin[:, None, :]
    return jnp.concatenate([x1 * c - x2 * s, x2 * c + x1 * s], axis=-1).astype(x.dtype)


def blocked_attention(qs, ks, vs):
    B, S = qs[0].shape[:2]
    nb = S // Q_BLOCK
    scale = qs[0].shape[-1] ** -0.5
    offs = np.cumsum([0] + [k.shape[1] for k in ks])

    def one(qb):
        s = jnp.concatenate([jnp.einsum('bqkgd,bskd->bkgqs', q, k) for q, k in zip(qb, ks)], axis=-1)
        prob = jax.nn.softmax(s.astype(jnp.float32) * scale, axis=-1).astype(vs[0].dtype)
        out = jnp.einsum('bkgqs,bskd->bqkgd', prob[..., offs[0]:offs[1]], vs[0])
        for i in range(1, len(vs)):
            out = out + jnp.einsum('bkgqs,bskd->bqkgd', prob[..., offs[i]:offs[i + 1]], vs[i])
        return out

    q_blocks = tuple(jnp.moveaxis(q.reshape(B, nb, Q_BLOCK, *q.shape[2:]), 1, 0) for q in qs)
    out = lax.map(one, q_blocks)
    return jnp.moveaxis(out, 0, 1).reshape(B, S, -1)


def neighbourhood_attention(q, k, v, k_ctx, v_ctx, rpb):
    B, S, H, dh = q.shape
    rows = S // GRID_W
    wr = min(NA_WIN_R, rows)
    n_cb = GRID_W // NA_WIN_C
    kc = 2 * NA_WIN_C
    lc = k_ctx.shape[1]
    scale = dh ** -0.5
    qcol = np.arange(GRID_W).reshape(n_cb, NA_WIN_C)
    kcol = np.clip(qcol[:, :1] - NA_WIN_C // 2, 0, GRID_W - kc) + np.arange(kc)
    cstart = np.clip(qcol - NA_WIN_C // 2, 0, GRID_W - NA_WIN_C)
    col_in = (kcol[:, None, :] >= cstart[:, :, None]) & (kcol[:, None, :] < cstart[:, :, None] + NA_WIN_C)
    dcol = np.clip(kcol[:, None, :] - qcol[:, :, None] + NA_WIN_C - 1, 0, 2 * NA_WIN_C - 2)
    rpb_cols = rpb.astype(jnp.float32)[:, :, dcol]
    q_g = q.reshape(B, rows, n_cb, NA_WIN_C, H, dh)
    k_g = k.reshape(B, rows, GRID_W, H, dh)
    v_g = v.reshape(B, rows, GRID_W, H, dh)

    def row_block(r):
        rs = jnp.clip(r - wr // 2, 0, rows - wr)
        qb = lax.dynamic_index_in_dim(q_g, r, axis=1, keepdims=False)
        kb = lax.dynamic_slice_in_dim(k_g, rs, wr, axis=1)[:, :, kcol]
        vb = lax.dynamic_slice_in_dim(v_g, rs, wr, axis=1)[:, :, kcol]
        bias = rpb_cols[:, rs - r + jnp.arange(wr) + NA_WIN_R - 1]
        s_win = jnp.einsum('bjqhd,bwjkhd->bjqhwk', qb, kb).astype(jnp.float32) * scale
        s_win = s_win + bias.transpose(2, 3, 0, 1, 4)
        s_win = jnp.where(col_in[:, :, None, None, :], s_win, NEG_INF)
        s_ctx = jnp.einsum('bjqhd,bshd->bjqhs', qb, k_ctx).astype(jnp.float32) * scale
        s = jnp.concatenate([s_ctx, s_win.reshape(B, n_cb, NA_WIN_C, H, wr * kc)], axis=-1)
        prob = jax.nn.softmax(s, axis=-1).astype(v.dtype)
        p_win = prob[..., lc:].reshape(B, n_cb, NA_WIN_C, H, wr, kc)
        return (jnp.einsum('bjqhs,bshd->bjqhd', prob[..., :lc], v_ctx)
                + jnp.einsum('bjqhwk,bwjkhd->bjqhd', p_win, vb))

    out = lax.map(row_block, jnp.arange(rows))
    return jnp.moveaxis(out, 0, 1).reshape(B, S, H * dh)


def short_conv(x, w):
    xp = jnp.pad(x, ((0, 0), (1, 1), (0, 0)))
    return xp[:, :-2] * w[0] + xp[:, 1:-1] * w[1] + xp[:, 2:] * w[2]


def spatial_gating(u, v, w_s, b_s, ln_g, ln_b):
    B, L, C = u.shape
    vn = layernorm(v, ln_g, ln_b).reshape(B, L // CHUNK, CHUNK, SGU_GROUPS, C // SGU_GROUPS)
    mixed = jnp.einsum('gts,bnsgc->bntgc', w_s, vn) + b_s.T[:, :, None]
    return u * mixed.reshape(B, L, C)


def moe_clamped_swiglu(h, w_router, b_router, w_gu, b_gu, w_down, b_down):
    shp = h.shape
    t = h.reshape(-1, shp[-1])
    n = t.shape[0]
    logits = (t @ w_router + b_router).astype(jnp.float32)
    top_val, top_idx = lax.top_k(logits, TOP_K)
    gate = jax.nn.softmax(top_val, axis=-1)
    n_assign = n * TOP_K
    flat_e = top_idx.reshape(-1)
    order = jnp.argsort(flat_e)
    e_sorted = flat_e[order]
    counts = jnp.bincount(flat_e, length=N_EXPERTS)
    n_blk_e = (counts + MOE_BLOCK - 1) // MOE_BLOCK
    blk_end = jnp.cumsum(n_blk_e)
    row_begin = jnp.cumsum(counts) - counts
    dest = (blk_end - n_blk_e)[e_sorted] * MOE_BLOCK + jnp.arange(n_assign) - row_begin[e_sorted]
    n_blocks = -(-n_assign // MOE_BLOCK) + N_EXPERTS
    n_slots = n_blocks * MOE_BLOCK
    slot_tok = jnp.full((n_slots,), n, jnp.int32).at[dest].set((order // TOP_K).astype(jnp.int32))
    slot_gate = jnp.zeros((n_slots,), jnp.float32).at[dest].set(gate.reshape(-1)[order])
    blk_expert = jnp.minimum(jnp.searchsorted(blk_end, jnp.arange(n_blocks), side='right'), N_EXPERTS - 1)
    t_pad = jnp.concatenate([t, jnp.zeros((1, t.shape[1]), t.dtype)], axis=0)

    def expert_block(args):
        tok, e = args
        xe = t_pad[tok]
        g, u = jnp.split(xe @ w_gu[e] + b_gu[e], 2, axis=-1)
        g = jnp.minimum(g, SWIGLU_LIMIT)
        u = jnp.clip(u, -SWIGLU_LIMIT, SWIGLU_LIMIT)
        return (g * jax.nn.sigmoid(SWIGLU_ALPHA * g) * (u + 1)) @ w_down[e] + b_down[e]

    yb = lax.map(expert_block, (slot_tok.reshape(n_blocks, MOE_BLOCK), blk_expert)).reshape(n_slots, -1)
    y = jax.ops.segment_sum(yb * slot_gate[:, None].astype(yb.dtype), slot_tok, num_segments=n + 1)[:n]
    return y.reshape(shp)


def trunk_layer(x, cvec, p, ctx=None):
    B, L, _ = x.shape
    mod = jax.nn.silu(cvec) @ p['w_ada'] + p['b_ada']
    shift1, scale1, gate1, shift2, scale2, gate2 = [m[:, None, :] for m in jnp.split(mod, 6, axis=-1)]
    h = rmsnorm(x) * (1 + scale1) + shift1
    split_at = np.cumsum(IN_WIDTHS)[:-1].tolist()
    (na_q, na_k, na_v, ga_q, ga_k, ga_v, sc_b, sc_c, sc_x, sg_u, sg_v) = jnp.split(h @ p['w_in'], split_at, axis=-1)
    na_q = na_q.reshape(B, L, N_NA_HEADS, HEAD_DIM)
    na_k = na_k.reshape(B, L, N_NA_HEADS, HEAD_DIM)
    na_v = na_v.reshape(B, L, N_NA_HEADS, HEAD_DIM)
    ga_q = rmsnorm(ga_q.reshape(B, L, N_GQA_HEADS, HEAD_DIM)) * p['q_norm']
    ga_k = rmsnorm(ga_k.reshape(B, L, N_KV_HEADS, HEAD_DIM)) * p['k_norm']
    ga_v = ga_v.reshape(B, L, N_KV_HEADS, HEAD_DIM)

    def group(q):
        return q.reshape(B, L, N_KV_HEADS, GQA_GROUP, HEAD_DIM)

    if ctx is None:
        y_na = blocked_attention([na_q[:, :, :, None, :]], [na_k], [na_v])
        y_ga = blocked_attention([group(ga_q)], [ga_k], [ga_v])
        ctx_out = (na_k, na_v, ga_k, ga_v)
    else:
        ck_na, cv_na, ck_ga, cv_ga = ctx
        y_na = neighbourhood_attention(na_q, na_k, na_v, ck_na, cv_na, p['na_rpb'])
        cos, sin = axial_rope_tables(L)
        y_ga = blocked_attention([group(ga_q), group(apply_rope(ga_q, cos, sin))],
                                 [ck_ga, apply_rope(ga_k, cos, sin)], [cv_ga, ga_v])
        ctx_out = None
    y_sc = sc_b * short_conv(sc_c * sc_x, p['conv_w'])
    y_sg = spatial_gating(sg_u, sg_v, p['sgu_w'], p['sgu_b'], p['sgu_ln_g'], p['sgu_ln_b'])
    merged = jnp.zeros_like(x)
    for i, y in enumerate((y_na, y_sc, y_sg, y_ga)):
        merged = merged + jax.nn.sigmoid(h @ p['w_gate'][i] + p['b_gate'][i]) * (y @ p['w_branch'][i])
    x = x + gate1 * (merged @ p['w_out'])
    h2 = rmsnorm(x) * (1 + scale2) + shift2
    x = x + gate2 * moe_clamped_swiglu(h2, p['w_router'], p['b_router'], p['w_gu'], p['b_gu'], p['w_down'], p['b_down'])
    return x, ctx_out


def setup_inputs(seed: int = 0) -> dict:
    key = jax.random.key(seed)
    keys = jax.random.split(key, 30)
    D = D_MODEL

    def nrm(i, shape, scale):
        return jax.random.normal(keys[i], shape, jnp.float32) * scale

    return {
        'x_prompt': nrm(0, (BATCH, SEQ, D), 1.0),
        'x_sample': nrm(1, (DEC_BATCH, DEC_SEQ, D), 1.0),
        'c': nrm(2, (DEC_BATCH, D), 1.0),
        'cache_na_k': nrm(3, (DEC_BATCH, DEPTH, PAST_LEN, N_NA_HEADS, HEAD_DIM), 1.0),
        'cache_na_v': nrm(4, (DEC_BATCH, DEPTH, PAST_LEN, N_NA_HEADS, HEAD_DIM), 1.0),
        'cache_gqa_k': nrm(5, (DEC_BATCH, DEPTH, PAST_LEN, N_KV_HEADS, HEAD_DIM), 1.0),
        'cache_gqa_v': nrm(6, (DEC_BATCH, DEPTH, PAST_LEN, N_KV_HEADS, HEAD_DIM), 1.0),
        'c_ctx': nrm(7, (D,), 1.0),
        'w_ada': nrm(8, (DEPTH, D, 6 * D), 0.2 * D ** -0.5),
        'b_ada': nrm(9, (DEPTH, 6 * D), 0.02),
        'w_in': nrm(10, (DEPTH, D, IN_WIDTH), D ** -0.5),
        'q_norm': 1.0 + nrm(11, (DEPTH, HEAD_DIM), 0.05),
        'k_norm': 1.0 + nrm(12, (DEPTH, HEAD_DIM), 0.05),
        'na_rpb': nrm(13, (DEPTH, N_NA_HEADS, 2 * NA_WIN_R - 1, 2 * NA_WIN_C - 1), 0.1),
        'conv_w': nrm(14, (DEPTH, CONV_K, CONV_WIDTH), CONV_K ** -0.5),
        'sgu_w': nrm(15, (DEPTH, SGU_GROUPS, CHUNK, CHUNK), CHUNK ** -0.5),
        'sgu_b': 1.0 + nrm(16, (DEPTH, SGU_GROUPS, CHUNK), 0.05),
        'sgu_ln_g': 1.0 + nrm(17, (DEPTH, SGU_WIDTH), 0.05),
        'sgu_ln_b': nrm(18, (DEPTH, SGU_WIDTH), 0.02),
        'w_branch': nrm(19, (DEPTH, N_BRANCH, BRANCH_WIDTH, D), BRANCH_WIDTH ** -0.5),
        'w_gate': nrm(20, (DEPTH, N_BRANCH, D, D), D ** -0.5),
        'b_gate': nrm(21, (DEPTH, N_BRANCH, D), 0.02),
        'w_out': nrm(22, (DEPTH, D, D), D ** -0.5),
        'w_router': nrm(23, (DEPTH, D, N_EXPERTS), D ** -0.5),
        'b_router': nrm(24, (DEPTH, N_EXPERTS), 0.01),
        'w_gu': nrm(25, (DEPTH, N_EXPERTS, D, 2 * EXPERT_FF), D ** -0.5),
        'b_gu': nrm(26, (DEPTH, N_EXPERTS, 2 * EXPERT_FF), 0.02),
        'w_down': nrm(27, (DEPTH, N_EXPERTS, EXPERT_FF, D), EXPERT_FF ** -0.5),
        'b_down': nrm(28, (DEPTH, N_EXPERTS, D), 0.02),
        'final_norm': 1.0 + nrm(29, (D,), 0.05),
    }


def reference(x_prompt, x_sample, c, cache_na_k, cache_na_v, cache_gqa_k, cache_gqa_v, c_ctx,
              w_ada, b_ada, w_in, q_norm, k_norm, na_rpb, conv_w, sgu_w, sgu_b, sgu_ln_g, sgu_ln_b,
              w_branch, w_gate, b_gate, w_out, w_router, b_router, w_gu, b_gu, w_down, b_down, final_norm):
    xp = x_prompt
    xs = x_sample
    st_na_k, st_na_v, st_ga_k, st_ga_v = [], [], [], []
    for l in range(DEPTH):
        p = {'w_ada': w_ada[l], 'b_ada': b_ada[l], 'w_in': w_in[l], 'q_norm': q_norm[l], 'k_norm': k_norm[l],
             'na_rpb': na_rpb[l], 'conv_w': conv_w[l], 'sgu_w': sgu_w[l], 'sgu_b': sgu_b[l],
             'sgu_ln_g': sgu_ln_g[l], 'sgu_ln_b': sgu_ln_b[l], 'w_branch': w_branch[l], 'w_gate': w_gate[l],
             'b_gate': b_gate[l], 'w_out': w_out[l], 'w_router': w_router[l], 'b_router': b_router[l],
             'w_gu': w_gu[l], 'b_gu': b_gu[l], 'w_down': w_down[l], 'b_down': b_down[l]}
        xp, (nk, nv, gk, gv) = trunk_layer(xp, c_ctx[None, :], p)
        st_na_k.append(nk)
        st_na_v.append(nv)
        st_ga_k.append(gk)
        st_ga_v.append(gv)
        xs, _ = trunk_layer(xs, c, p, (cache_na_k[:, l], cache_na_v[:, l], cache_gqa_k[:, l], cache_gqa_v[:, l]))
    y_prompt = rmsnorm(xp) * final_norm
    y_sample = rmsnorm(xs) * final_norm
    state_na_k = jnp.stack(st_na_k, axis=1)
    state_na_v = jnp.stack(st_na_v, axis=1)
    state_gqa_k = jnp.stack(st_ga_k, axis=1)
    state_gqa_v = jnp.stack(st_ga_v, axis=1)
    return (y_prompt, y_sample, state_na_k, state_na_v, state_gqa_k, state_gqa_v)
```

```python
import functools

import numpy as np
import jax
import jax.numpy as jnp
from jax import lax
from jax.experimental import pallas as pl
from jax.experimental.pallas import tpu as pltpu

F32 = jnp.float32
BF16 = jnp.bfloat16

LANES = 128
HEAD_DIM = 128
GRID_W = 64
NA_WIN_R = 8
NA_WIN_C = 16
N_KV_HEADS = 2
ROPE_THETA = 10000.0
CHUNK = 128
TOP_K = 4
SWIGLU_ALPHA = 1.702
SWIGLU_LIMIT = 7.0
MOE_BLOCK = 256
NORM_EPS = 1e-6
NEG_INF = -1e30
ATT_SCALE = HEAD_DIM ** -0.5
NA_QROWS = 4
NA_KROWS = 12
VMEM_LIMIT = 56 * 1024 * 1024


def _cparams(sem):
    return pltpu.CompilerParams(dimension_semantics=sem, vmem_limit_bytes=VMEM_LIMIT)


def _pick(n, prefs):
    for p in prefs:
        if n % p == 0:
            return p
    return n


def _tile_mod_rows(tm, n_prompt, n_total, seq_sample):
    starts = np.arange(0, n_total, tm)
    return jnp.asarray(np.where(starts < n_prompt, 0, 1 + (starts - n_prompt) // seq_sample), jnp.int32)


def _rms(x):
    return x * lax.rsqrt(jnp.mean(x * x, axis=-1, keepdims=True) + NORM_EPS)


def _ada_kernel(cv_ref, w_ref, b_ref, o_ref):
    cv = cv_ref[...]
    a = (cv * jax.nn.sigmoid(cv)).astype(BF16)
    o_ref[0] = jnp.dot(a, w_ref[0].astype(BF16), preferred_element_type=F32) + b_ref[0]


def _ada_table(cv, w_ada, b_ada):
    depth, d, n = w_ada.shape
    tn = _pick(n, (512, 256, 128))
    return pl.pallas_call(
        _ada_kernel,
        grid=(depth, n // tn),
        in_specs=[pl.BlockSpec((8, d), lambda l, j: (0, 0)),
                  pl.BlockSpec((1, d, tn), lambda l, j: (l, 0, j)),
                  pl.BlockSpec((1, 1, tn), lambda l, j: (l, 0, j))],
        out_specs=pl.BlockSpec((1, 8, tn), lambda l, j: (l, 0, j)),
        out_shape=jax.ShapeDtypeStruct((depth, 8, n), F32),
        compiler_params=_cparams(("parallel", "parallel")),
        name="ada_table",
    )(cv, w_ada, b_ada.reshape(depth, 1, n))


def _normmod_kernel(rows_ref, x_ref, scale_ref, shift_ref, o_ref):
    del rows_ref
    o_ref[...] = (_rms(x_ref[...]) * (1.0 + scale_ref[0]) + shift_ref[0]).astype(o_ref.dtype)


def _normmod(x, mod, rows, tm, which_scale, which_shift, out_dtype):
    t, d = x.shape
    return pl.pallas_call(
        _normmod_kernel,
        grid_spec=pltpu.PrefetchScalarGridSpec(
            num_scalar_prefetch=1, grid=(t // tm,),
            in_specs=[pl.BlockSpec((tm, d), lambda i, r: (i, 0)),
                      pl.BlockSpec((1, 1, d), lambda i, r: (r[i] * 6 + which_scale, 0, 0)),
                      pl.BlockSpec((1, 1, d), lambda i, r: (r[i] * 6 + which_shift, 0, 0))],
            out_specs=pl.BlockSpec((tm, d), lambda i, r: (i, 0))),
        out_shape=jax.ShapeDtypeStruct((t, d), out_dtype),
        compiler_params=_cparams(("parallel",)),
        name="normmod",
    )(rows, x, mod, mod)


def _mm_kernel(a_ref, w_ref, o_ref):
    o_ref[...] = jnp.dot(a_ref[...], w_ref[...], preferred_element_type=F32).astype(o_ref.dtype)


def _matmul(a, w, out_dtype, tm, tn):
    m, k = a.shape
    n = w.shape[1]
    return pl.pallas_call(
        _mm_kernel,
        grid=(m // tm, n // tn),
        in_specs=[pl.BlockSpec((tm, k), lambda i, j: (i, 0)),
                  pl.BlockSpec((k, tn), lambda i, j: (0, j))],
        out_specs=pl.BlockSpec((tm, tn), lambda i, j: (i, j)),
        out_shape=jax.ShapeDtypeStruct((m, n), out_dtype),
        compiler_params=_cparams(("parallel", "parallel")),
        name="in_proj",
    )(a, w)


def _outproj_kernel(rows_ref, a_ref, w_ref, x_ref, g_ref, o_ref):
    del rows_ref
    o_ref[...] = x_ref[...] + g_ref[0] * jnp.dot(a_ref[...], w_ref[...], preferred_element_type=F32)


def _outproj(a, w, x, mod, rows, tm, tn):
    m, k = a.shape
    n = w.shape[1]
    return pl.pallas_call(
        _outproj_kernel,
        grid_spec=pltpu.PrefetchScalarGridSpec(
            num_scalar_prefetch=1, grid=(m // tm, n // tn),
            in_specs=[pl.BlockSpec((tm, k), lambda i, j, r: (i, 0)),
                      pl.BlockSpec((k, tn), lambda i, j, r: (0, j)),
                      pl.BlockSpec((tm, tn), lambda i, j, r: (i, j)),
                      pl.BlockSpec((1, 1, tn), lambda i, j, r: (r[i] * 6 + 2, 0, j))],
            out_specs=pl.BlockSpec((tm, tn), lambda i, j, r: (i, j))),
        out_shape=jax.ShapeDtypeStruct((m, n), F32),
        compiler_params=_cparams(("parallel", "parallel")),
        name="out_proj",
    )(rows, a, w, x, mod)


def _gatebranch_kernel(h_ref, wg_ref, bg_ref, y_ref, wb_ref, o_ref, acc_ref):
    br = pl.program_id(2)
    g = jnp.dot(h_ref[...], wg_ref[0], preferred_element_type=F32) + bg_ref[0]
    p = jnp.dot(y_ref[0], wb_ref[0], preferred_element_type=F32)
    v = jax.nn.sigmoid(g) * p

    @pl.when(br == 0)
    def _():
        acc_ref[...] = v

    @pl.when(br > 0)
    def _():
        acc_ref[...] += v

    @pl.when(br == pl.num_programs(2) - 1)
    def _():
        o_ref[...] = acc_ref[...].astype(o_ref.dtype)


def _gatebranch(h, w_gate, b_gate, ys, w_branch, tm, tn):
    t, d = h.shape
    nbr, bw, _ = w_branch.shape
    return pl.pallas_call(
        _gatebranch_kernel,
        grid=(t // tm, d // tn, nbr),
        in_specs=[pl.BlockSpec((tm, d), lambda i, j, b: (i, 0)),
                  pl.BlockSpec((1, d, tn), lambda i, j, b: (b, 0, j)),
                  pl.BlockSpec((1, 1, tn), lambda i, j, b: (b, 0, j)),
                  pl.BlockSpec((1, tm, bw), lambda i, j, b: (b, i, 0)),
                  pl.BlockSpec((1, bw, tn), lambda i, j, b: (b, 0, j))],
        out_specs=pl.BlockSpec((tm, tn), lambda i, j, b: (i, j)),
        out_shape=jax.ShapeDtypeStruct((t, d), BF16),
        scratch_shapes=[pltpu.VMEM((tm, tn), F32)],
        compiler_params=_cparams(("parallel", "parallel", "arbitrary")),
        name="gate_branch",
    )(h, w_gate, b_gate.reshape(nbr, 1, d), ys, w_branch)


def _softmax_parts(scores):
    m = scores[0].max(axis=-1, keepdims=True)
    for s in scores[1:]:
        m = jnp.maximum(m, s.max(axis=-1, keepdims=True))
    ps = [jnp.exp(s - m) for s in scores]
    l = ps[0].sum(axis=-1, keepdims=True)
    for p in ps[1:]:
        l = l + p.sum(axis=-1, keepdims=True)
    return ps, l


def _qk(q, k):
    return lax.dot_general(q, k, (((1,), (1,)), ((), ())), preferred_element_type=F32) * ATT_SCALE


def _ctx_na_kernel(q_ref, k_ref, v_ref, o_ref):
    s = _qk(q_ref[...].astype(BF16), k_ref[...].astype(BF16))
    (p,), l = _softmax_parts([s])
    o_ref[...] = (jnp.dot(p.astype(BF16), v_ref[...].astype(BF16), preferred_element_type=F32) / l).astype(o_ref.dtype)


def _ctx_na_attention(proj, nb, seq, heads, cq, ck, cv):
    return pl.pallas_call(
        _ctx_na_kernel,
        grid=(nb, heads),
        in_specs=[pl.BlockSpec((seq, HEAD_DIM), lambda b, h: (b, cq + h)),
                  pl.BlockSpec((seq, HEAD_DIM), lambda b, h: (b, ck + h)),
                  pl.BlockSpec((seq, HEAD_DIM), lambda b, h: (b, cv + h))],
        out_specs=pl.BlockSpec((seq, HEAD_DIM), lambda b, h: (b, h)),
        out_shape=jax.ShapeDtypeStruct((nb * seq, heads * HEAD_DIM), BF16),
        compiler_params=_cparams(("parallel", "parallel")),
        name="ctx_na_attn",
    )(proj, proj, proj)


def _ctx_gqa_kernel(q_ref, k_ref, v_ref, qn_ref, kn_ref, o_ref, ko_ref):
    kn = _rms(k_ref[...]) * kn_ref[...]
    ko_ref[...] = kn
    qn = _rms(q_ref[...]) * qn_ref[...]
    s = _qk(qn.astype(BF16), kn.astype(BF16))
    (p,), l = _softmax_parts([s])
    o_ref[...] = (jnp.dot(p.astype(BF16), v_ref[...].astype(BF16), preferred_element_type=F32) / l).astype(o_ref.dtype)


def _ctx_gqa_attention(proj, q_norm, k_norm, nb, seq, heads, group, cq, ck, cv):
    return pl.pallas_call(
        _ctx_gqa_kernel,
        grid=(nb, heads),
        in_specs=[pl.BlockSpec((seq, HEAD_DIM), lambda b, h: (b, cq + h)),
                  pl.BlockSpec((seq, HEAD_DIM), lambda b, h: (b, ck + h // group)),
                  pl.BlockSpec((seq, HEAD_DIM), lambda b, h: (b, cv + h // group)),
                  pl.BlockSpec((1, HEAD_DIM), lambda b, h: (0, 0)),
                  pl.BlockSpec((1, HEAD_DIM), lambda b, h: (0, 0))],
        out_specs=[pl.BlockSpec((seq, HEAD_DIM), lambda b, h: (b, h)),
                   pl.BlockSpec((seq, HEAD_DIM), lambda b, h: (b, h // group))],
        out_shape=[jax.ShapeDtypeStruct((nb * seq, heads * HEAD_DIM), BF16),
                   jax.ShapeDtypeStruct((nb * seq, (heads // group) * HEAD_DIM), F32)],
        compiler_params=_cparams(("parallel", "arbitrary")),
        name="ctx_gqa_attn",
    )(proj, proj, proj, q_norm.reshape(1, HEAD_DIM), k_norm.reshape(1, HEAD_DIM))


def _rope(x, cos_t, sin_t):
    return x * cos_t + pltpu.roll(x, HEAD_DIM // 2, 1) * sin_t


def _lat_gqa_kernel(q_ref, k_ref, v_ref, kc_ref, vc_ref, cos_ref, sin_ref, qn_ref, kn_ref, o_ref, kr_ref, *, tq):
    qi = pl.program_id(2)

    @pl.when(qi == 0)
    def _():
        kn = _rms(k_ref[...]) * kn_ref[...]
        kr_ref[...] = _rope(kn, cos_ref[...], sin_ref[...]).astype(BF16)

    rows = pl.ds(pl.multiple_of(qi * tq, tq), tq)
    qn = _rms(q_ref[...]) * qn_ref[...]
    qr = _rope(qn, cos_ref[rows, :], sin_ref[rows, :])
    s_ctx = _qk(qn.astype(BF16), kc_ref[0, 0].astype(BF16))
    s_lat = _qk(qr.astype(BF16), kr_ref[...])
    (p_ctx, p_lat), l = _softmax_parts([s_ctx, s_lat])
    out = (jnp.dot(p_ctx.astype(BF16), vc_ref[0, 0].astype(BF16), preferred_element_type=F32)
           + jnp.dot(p_lat.astype(BF16), v_ref[...].astype(BF16), preferred_element_type=F32))
    o_ref[...] = (out / l).astype(o_ref.dtype)


def _lat_gqa_attention(proj, cache_k, cache_v, layer, cos_t, sin_t, q_norm, k_norm,
                       row0, nb, seq, heads, group, cq, ck, cv):
    tq = _pick(seq, (512, 256, 128))
    nq = seq // tq
    rb = row0 // tq
    sb = row0 // seq
    past = cache_k.shape[2]
    return pl.pallas_call(
        functools.partial(_lat_gqa_kernel, tq=tq),
        grid=(nb, heads, nq),
        in_specs=[pl.BlockSpec((tq, HEAD_DIM), lambda b, h, i: (rb + b * nq + i, cq + h)),
                  pl.BlockSpec((seq, HEAD_DIM), lambda b, h, i: (sb + b, ck + h // group)),
                  pl.BlockSpec((seq, HEAD_DIM), lambda b, h, i: (sb + b, cv + h // group)),
                  pl.BlockSpec((1, 1, past, HEAD_DIM), lambda b, h, i: (b, layer, 0, h // group)),
                  pl.BlockSpec((1, 1, past, HEAD_DIM), lambda b, h, i: (b, layer, 0, h // group)),
                  pl.BlockSpec((seq, HEAD_DIM), lambda b, h, i: (0, 0)),
                  pl.BlockSpec((seq, HEAD_DIM), lambda b, h, i: (0, 0)),
                  pl.BlockSpec((1, HEAD_DIM), lambda b, h, i: (0, 0)),
                  pl.BlockSpec((1, HEAD_DIM), lambda b, h, i: (0, 0))],
        out_specs=pl.BlockSpec((tq, HEAD_DIM), lambda b, h, i: (b * nq + i, h)),
        out_shape=jax.ShapeDtypeStruct((nb * seq, heads * HEAD_DIM), BF16),
        scratch_shapes=[pltpu.VMEM((seq, HEAD_DIM), BF16)],
        compiler_params=_cparams(("parallel", "arbitrary", "arbitrary")),
        name="lat_gqa_attn",
    )(proj, proj, proj, cache_k, cache_v, cos_t, sin_t, q_norm.reshape(1, HEAD_DIM), k_norm.reshape(1, HEAD_DIM))


def _na_key_row0(i, n_rows):
    return jnp.clip(NA_QROWS * i - NA_WIN_R // 2, 0, n_rows - NA_KROWS)


def _lat_na_kernel(q_ref, k_ref, v_ref, kc_ref, vc_ref, bias_ref, o_ref, *, n_rows):
    i = pl.program_id(2)
    start = pl.multiple_of(_na_key_row0(i, n_rows) * GRID_W, GRID_W)
    keys = pl.ds(start, NA_KROWS * GRID_W)
    q = q_ref[...].astype(BF16)
    s_ctx = _qk(q, kc_ref[0, 0].astype(BF16))
    s_win = _qk(q, k_ref[keys, :].astype(BF16)) + bias_ref[0, 0]
    (p_ctx, p_win), l = _softmax_parts([s_ctx, s_win])
    out = (jnp.dot(p_ctx.astype(BF16), vc_ref[0, 0].astype(BF16), preferred_element_type=F32)
           + jnp.dot(p_win.astype(BF16), v_ref[keys, :].astype(BF16), preferred_element_type=F32))
    o_ref[...] = (out / l).astype(o_ref.dtype)


def _na_bias_tables(rpb, n_rows):
    n_blocks = n_rows // NA_QROWS
    tabs_r, tabs_c, tabs_ok = [], [], []
    for i in (0, min(2, n_blocks - 1), n_blocks - 1):
        k0 = int(np.clip(NA_QROWS * i - NA_WIN_R // 2, 0, n_rows - NA_KROWS))
        qr = NA_QROWS * i + np.arange(NA_QROWS)[:, None, None, None]
        qc = np.arange(GRID_W)[None, :, None, None]
        kr = k0 + np.arange(NA_KROWS)[None, None, :, None]
        kc = np.arange(GRID_W)[None, None, None, :]
        rs = np.clip(qr - NA_WIN_R // 2, 0, n_rows - NA_WIN_R)
        cs = np.clip(qc - NA_WIN_C // 2, 0, GRID_W - NA_WIN_C)
        ok = (kr >= rs) & (kr < rs + NA_WIN_R) & (kc >= cs) & (kc < cs + NA_WIN_C)
        ridx = np.clip(kr - qr + NA_WIN_R - 1, 0, 2 * NA_WIN_R - 2)
        cidx = np.clip(kc - qc + NA_WIN_C - 1, 0, 2 * NA_WIN_C - 2)
        shape = (NA_QROWS * GRID_W, NA_KROWS * GRID_W)
        tabs_r.append(np.broadcast_to(ridx, ok.shape).reshape(shape))
        tabs_c.append(np.broadcast_to(cidx, ok.shape).reshape(shape))
        tabs_ok.append(ok.reshape(shape))
    ridx, cidx, ok = np.stack(tabs_r), np.stack(tabs_c), np.stack(tabs_ok)
    return jnp.where(ok[None], rpb.astype(F32)[:, ridx, cidx], NEG_INF)


def _lat_na_attention(proj, cache_k, cache_v, layer, bias, row0, nb, seq, heads, cq, ck, cv):
    n_rows = seq // GRID_W
    tq = NA_QROWS * GRID_W
    nq = seq // tq
    rb = row0 // tq
    sb = row0 // seq
    past = cache_k.shape[2]
    kw = NA_KROWS * GRID_W

    def bias_map(b, h, i):
        return (h, jnp.where(i == 0, 0, jnp.where(i == nq - 1, 2, 1)), 0, 0)

    return pl.pallas_call(
        functools.partial(_lat_na_kernel, n_rows=n_rows),
        grid=(nb, heads, nq),
        in_specs=[pl.BlockSpec((tq, HEAD_DIM), lambda b, h, i: (rb + b * nq + i, cq + h)),
                  pl.BlockSpec((seq, HEAD_DIM), lambda b, h, i: (sb + b, ck + h)),
                  pl.BlockSpec((seq, HEAD_DIM), lambda b, h, i: (sb + b, cv + h)),
                  pl.BlockSpec((1, 1, past, HEAD_DIM), lambda b, h, i: (b, layer, 0, h)),
                  pl.BlockSpec((1, 1, past, HEAD_DIM), lambda b, h, i: (b, layer, 0, h)),
                  pl.BlockSpec((1, 1, tq, kw), bias_map)],
        out_specs=pl.BlockSpec((tq, HEAD_DIM), lambda b, h, i: (b * nq + i, h)),
        out_shape=jax.ShapeDtypeStruct((nb * seq, heads * HEAD_DIM), BF16),
        compiler_params=_cparams(("parallel", "parallel", "arbitrary")),
        name="lat_na_attn",
    )(proj, proj, proj, cache_k, cache_v, bias)


def _conv_kernel(b_ref, c_ref, x_ref, w_ref, o_ref):
    p = c_ref[...] * x_ref[...]
    n = p.shape[0]
    row = lax.broadcasted_iota(jnp.int32, p.shape, 0)
    prev = jnp.where(row == 0, 0.0, pltpu.roll(p, 1, 0))
    nxt = jnp.where(row == n - 1, 0.0, pltpu.roll(p, n - 1, 0))
    w = w_ref[...]
    o_ref[...] = (b_ref[...] * (prev * w[0:1] + p * w[1:2] + nxt * w[2:3])).astype(o_ref.dtype)


def _short_conv(proj, conv_w, row0, nb, seq, width, off_b, off_c, off_x):
    tc = _pick(width, (256, 128))
    assert off_b % tc == 0 and off_c % tc == 0 and off_x % tc == 0
    sb = row0 // seq
    k = width // tc
    cb, cc, cx = off_b // tc, off_c // tc, off_x // tc
    return pl.pallas_call(
        _conv_kernel,
        grid=(nb, k),
        in_specs=[pl.BlockSpec((seq, tc), lambda b, j: (sb + b, cb + j)),
                  pl.BlockSpec((seq, tc), lambda b, j: (sb + b, cc + j)),
                  pl.BlockSpec((seq, tc), lambda b, j: (sb + b, cx + j)),
                  pl.BlockSpec((3, tc), lambda b, j: (0, j))],
        out_specs=pl.BlockSpec((seq, tc), lambda b, j: (b, j)),
        out_shape=jax.ShapeDtypeStruct((nb * seq, width), BF16),
        compiler_params=_cparams(("parallel", "parallel")),
        name="short_conv",
    )(proj, proj, proj, conv_w)


def _sgu_kernel(*refs, pieces):
    u_refs, v_refs = refs[:pieces], refs[pieces:2 * pieces]
    g_ref, b_ref, w_ref, bt_ref, o_ref = refs[2 * pieces:]
    rows, width = o_ref.shape
    groups = w_ref.shape[0]
    gw = width // groups
    pw = width // pieces
    for c in range(rows // CHUNK):
        rs = slice(c * CHUNK, (c + 1) * CHUNK)
        v = jnp.concatenate([r[rs, :] for r in v_refs], axis=-1)
        mu = jnp.mean(v, axis=-1, keepdims=True)
        var = jnp.mean(jnp.square(v - mu), axis=-1, keepdims=True)
        vn = ((v - mu) * lax.rsqrt(var + NORM_EPS)) * g_ref[...] + b_ref[...]
        for g in range(groups):
            cs = slice(g * gw, (g + 1) * gw)
            mixed = jnp.dot(w_ref[g].astype(BF16), vn[:, cs].astype(BF16), preferred_element_type=F32)
            mixed = mixed + bt_ref[:, g:g + 1]
            piece, lo = divmod(g * gw, pw)
            o_ref[rs, cs] = (u_refs[piece][rs, lo:lo + gw] * mixed).astype(o_ref.dtype)


def _spatial_gating(proj, ln_g, ln_b, w_s, b_s, width, off_u, off_v):
    t = proj.shape[0]
    rows = _pick(t, (256, 128))
    groups = w_s.shape[0]
    pw = int(np.gcd.reduce([width, off_u, off_v]))
    pieces = width // pw
    assert pw % (width // groups) == 0
    col_specs = [pl.BlockSpec((rows, pw), functools.partial(lambda i, c: (i, c), c=off // pw + p))
                 for off in (off_u, off_v) for p in range(pieces)]
    return pl.pallas_call(
        functools.partial(_sgu_kernel, pieces=pieces),
        grid=(t // rows,),
        in_specs=col_specs + [pl.BlockSpec((1, width), lambda i: (0, 0)),
                              pl.BlockSpec((1, width), lambda i: (0, 0)),
                              pl.BlockSpec((groups, CHUNK, CHUNK), lambda i: (0, 0, 0)),
                              pl.BlockSpec((CHUNK, groups), lambda i: (0, 0))],
        out_specs=pl.BlockSpec((rows, width), lambda i: (i, 0)),
        out_shape=jax.ShapeDtypeStruct((t, width), BF16),
        compiler_params=_cparams(("parallel",)),
        name="spatial_gating",
    )(*([proj] * (2 * pieces)), ln_g.reshape(1, width), ln_b.reshape(1, width), w_s, b_s.T)


def _router_kernel(rows_ref, x_ref, scale_ref, shift_ref, wr_ref, br_ref,
                   h_ref, idx_ref, gate_ref, rank_ref, cnt_ref, carry_ref):
    del rows_ref
    i = pl.program_id(0)

    @pl.when(i == 0)
    def _():
        carry_ref[...] = jnp.zeros_like(carry_ref)

    h = _rms(x_ref[...]) * (1.0 + scale_ref[0]) + shift_ref[0]
    h_ref[...] = h
    logits = jnp.dot(h, wr_ref[...], preferred_element_type=F32, precision=lax.Precision.HIGHEST) + br_ref[...]
    tm, n_exp = logits.shape
    lane_e = lax.broadcasted_iota(jnp.int32, (tm, n_exp), 1)
    lane_o = lax.broadcasted_iota(jnp.int32, (tm, LANES), 1)

    work = logits
    vals, sels = [], []
    for _ in range(TOP_K):
        m = work.max(axis=-1, keepdims=True)
        sel = jnp.min(jnp.where(work == m, lane_e, n_exp), axis=-1, keepdims=True)
        vals.append(m)
        sels.append(sel)
        work = jnp.where(lane_e == sel, -jnp.inf, work)
    exps = [jnp.exp(v - vals[0]) for v in vals]
    denom = exps[0]
    for e in exps[1:]:
        denom = denom + e

    onehot = jnp.zeros((tm, n_exp), F32)
    for sel in sels:
        onehot = onehot + (lane_e == sel).astype(F32)
    r = lax.broadcasted_iota(jnp.int32, (tm, tm), 0)
    c = lax.broadcasted_iota(jnp.int32, (tm, tm), 1)
    before = (r > c).astype(BF16)
    prefix = jnp.dot(before, onehot.astype(BF16), preferred_element_type=F32) + carry_ref[...]
    carry = carry_ref[...] + onehot.sum(axis=0, keepdims=True)
    carry_ref[...] = carry
    cnt_ref[...] = carry

    idx_out = jnp.zeros((tm, LANES), jnp.int32)
    gate_out = jnp.zeros((tm, LANES), F32)
    rank_out = jnp.zeros((tm, LANES), jnp.int32)
    for k in range(TOP_K):
        rank = jnp.sum(jnp.where(lane_e == sels[k], prefix, 0.0), axis=-1, keepdims=True).astype(jnp.int32)
        idx_out = jnp.where(lane_o == k, sels[k], idx_out)
        gate_out = jnp.where(lane_o == k, exps[k] / denom, gate_out)
        rank_out = jnp.where(lane_o == k, rank, rank_out)
    idx_ref[...] = idx_out
    gate_ref[...] = gate_out
    rank_ref[...] = rank_out


def _router(x, mod, rows, tm, w_router, b_router):
    t, d = x.shape
    n_exp = w_router.shape[1]
    return pl.pallas_call(
        _router_kernel,
        grid_spec=pltpu.PrefetchScalarGridSpec(
            num_scalar_prefetch=1, grid=(t // tm,),
            in_specs=[pl.BlockSpec((tm, d), lambda i, r: (i, 0)),
                      pl.BlockSpec((1, 1, d), lambda i, r: (r[i] * 6 + 4, 0, 0)),
                      pl.BlockSpec((1, 1, d), lambda i, r: (r[i] * 6 + 3, 0, 0)),
                      pl.BlockSpec((d, n_exp), lambda i, r: (0, 0)),
                      pl.BlockSpec((1, n_exp), lambda i, r: (0, 0))],
            out_specs=[pl.BlockSpec((tm, d), lambda i, r: (i, 0)),
                       pl.BlockSpec((tm, LANES), lambda i, r: (i, 0)),
                       pl.BlockSpec((tm, LANES), lambda i, r: (i, 0)),
                       pl.BlockSpec((tm, LANES), lambda i, r: (i, 0)),
                       pl.BlockSpec((1, n_exp), lambda i, r: (0, 0))],
            scratch_shapes=[pltpu.VMEM((1, n_exp), F32)]),
        out_shape=[jax.ShapeDtypeStruct((t, d), F32),
                   jax.ShapeDtypeStruct((t, LANES), jnp.int32),
                   jax.ShapeDtypeStruct((t, LANES), F32),
                   jax.ShapeDtypeStruct((t, LANES), jnp.int32),
                   jax.ShapeDtypeStruct((1, n_exp), F32)],
        compiler_params=_cparams(("arbitrary",)),
        name="router",
    )(rows, x, mod, mod, w_router, b_router.reshape(1, n_exp))


def _gather_kernel(tok_ref, src_ref, o_ref, buf_ref, sem):
    blk = pl.program_id(0)
    n_rows = o_ref.shape[0]
    sub = src_ref.shape[1]

    def row_copy(r, tok):
        return pltpu.make_async_copy(src_ref.at[tok], buf_ref.at[pl.ds(pl.multiple_of(r * sub, sub), sub), :], sem)

    def issue(r, carry):
        row_copy(r, tok_ref[blk * n_rows + r]).start()
        return carry

    def drain(r, carry):
        row_copy(r, 0).wait()
        return carry

    lax.fori_loop(0, n_rows, issue, 0)
    lax.fori_loop(0, n_rows, drain, 0)
    for s in range(sub):
        o_ref[:, s * LANES:(s + 1) * LANES] = buf_ref[pl.ds(s, n_rows, stride=sub), :].astype(o_ref.dtype)


def _gather_rows(slot_tok, src3, n_slots):
    _, sub, _ = src3.shape
    d = sub * LANES
    return pl.pallas_call(
        _gather_kernel,
        grid_spec=pltpu.PrefetchScalarGridSpec(
            num_scalar_prefetch=1, grid=(n_slots // MOE_BLOCK,),
            in_specs=[pl.BlockSpec(memory_space=pl.ANY)],
            out_specs=pl.BlockSpec((MOE_BLOCK, d), lambda i, t: (i, 0)),
            scratch_shapes=[pltpu.VMEM((MOE_BLOCK * sub, LANES), F32), pltpu.SemaphoreType.DMA(())]),
        out_shape=jax.ShapeDtypeStruct((n_slots, d), BF16),
        compiler_params=_cparams(("arbitrary",)),
        name="expert_gather",
    )(slot_tok, src3)


def _expert_up_kernel(we_ref, wt_ref, wb_ref, wo_ref, flag_ref, x_ref, wg_ref, wu_ref, bg_ref, bu_ref, o_ref,
                      wg_bf, wu_bf):
    del we_ref, wt_ref, wb_ref, wo_ref
    flag = flag_ref[pl.program_id(0)]

    @pl.when(flag == ITEM_NEW_WEIGHTS)
    def _():
        wg_bf[...] = wg_ref[0, 0].astype(BF16)
        wu_bf[...] = wu_ref[0, 0].astype(BF16)

    @pl.when(flag != ITEM_FILL)
    def _():
        x = x_ref[...]
        g = jnp.dot(x, wg_bf[...], preferred_element_type=F32) + bg_ref[0, 0]
        u = jnp.dot(x, wu_bf[...], preferred_element_type=F32) + bu_ref[0, 0]
        g = jnp.minimum(g, SWIGLU_LIMIT)
        u = jnp.clip(u, -SWIGLU_LIMIT, SWIGLU_LIMIT)
        o_ref[...] = (g * jax.nn.sigmoid(SWIGLU_ALPHA * g) * (u + 1.0)).astype(o_ref.dtype)

    @pl.when(flag == ITEM_FILL)
    def _():
        o_ref[...] = jnp.zeros_like(o_ref)


def _expert_up(sched, xs, w_gu, b_gu4, layer, tf):
    n_slots, d = xs.shape
    ff = w_gu.shape[3] // 2
    nf = ff // tf
    n_items = sched[0].shape[0]
    return pl.pallas_call(
        _expert_up_kernel,
        grid_spec=pltpu.PrefetchScalarGridSpec(
            num_scalar_prefetch=5, grid=(n_items,),
            in_specs=[pl.BlockSpec((MOE_BLOCK, d), lambda w, e, t, b, o, s: (b[w], 0)),
                      pl.BlockSpec((1, 1, d, tf), lambda w, e, t, b, o, s: (layer, e[w], 0, t[w])),
                      pl.BlockSpec((1, 1, d, tf), lambda w, e, t, b, o, s: (layer, e[w], 0, nf + t[w])),
                      pl.BlockSpec((1, 1, 1, tf), lambda w, e, t, b, o, s: (layer, e[w], 0, t[w])),
                      pl.BlockSpec((1, 1, 1, tf), lambda w, e, t, b, o, s: (layer, e[w], 0, nf + t[w]))],
            out_specs=pl.BlockSpec((MOE_BLOCK, tf), lambda w, e, t, b, o, s: (b[w], o[w])),
            scratch_shapes=[pltpu.VMEM((d, tf), BF16), pltpu.VMEM((d, tf), BF16)]),
        out_shape=jax.ShapeDtypeStruct((n_slots, ff), BF16),
        compiler_params=_cparams(("arbitrary",)),
        name="expert_up",
    )(*sched, xs, w_gu, w_gu, b_gu4, b_gu4)


def _expert_down_kernel(we_ref, wt_ref, wb_ref, wo_ref, flag_ref, h_ref, wd_ref, bd_ref, o_ref, wd_bf):
    del we_ref, wt_ref, wb_ref, wo_ref
    flag = flag_ref[pl.program_id(0)]

    @pl.when(flag == ITEM_NEW_WEIGHTS)
    def _():
        wd_bf[...] = wd_ref[0, 0].astype(BF16)

    @pl.when(flag != ITEM_FILL)
    def _():
        o_ref[...] = jnp.dot(h_ref[...], wd_bf[...], preferred_element_type=F32) + bd_ref[0, 0]

    @pl.when(flag == ITEM_FILL)
    def _():
        o_ref[...] = jnp.zeros_like(o_ref)


def _expert_down(sched, hs, w_down, b_down4, layer, tn):
    n_slots, ff = hs.shape
    d = w_down.shape[3]
    n_items = sched[0].shape[0]
    return pl.pallas_call(
        _expert_down_kernel,
        grid_spec=pltpu.PrefetchScalarGridSpec(
            num_scalar_prefetch=5, grid=(n_items,),
            in_specs=[pl.BlockSpec((MOE_BLOCK, ff), lambda w, e, t, b, o, s: (b[w], 0)),
                      pl.BlockSpec((1, 1, ff, tn), lambda w, e, t, b, o, s: (layer, e[w], 0, t[w])),
                      pl.BlockSpec((1, 1, 1, tn), lambda w, e, t, b, o, s: (layer, e[w], 0, t[w]))],
            out_specs=pl.BlockSpec((MOE_BLOCK, tn), lambda w, e, t, b, o, s: (b[w], o[w])),
            scratch_shapes=[pltpu.VMEM((ff, tn), BF16)]),
        out_shape=jax.ShapeDtypeStruct((n_slots, d), F32),
        compiler_params=_cparams(("arbitrary",)),
        name="expert_down",
    )(*sched, hs, w_down, b_down4)


ITEM_FILL, ITEM_COMPUTE, ITEM_NEW_WEIGHTS = 0, 1, 2


def _expert_schedule(n_blk_e, blk_start, n_tiles, n_blocks):
    i32 = jnp.int32
    items = n_blk_e * n_tiles
    item_end = jnp.cumsum(items)
    item_start = item_end - items
    n_used = jnp.sum(n_blk_e)
    total = n_used * n_tiles
    w = jnp.arange(n_tiles * n_blocks, dtype=i32)
    wc = jnp.minimum(w, total - 1)
    e = jnp.minimum(jnp.searchsorted(item_end, wc, side='right'), n_blk_e.shape[0] - 1).astype(i32)
    local = wc - item_start[e]
    nbe = jnp.maximum(n_blk_e[e], 1)
    tile = local // nbe
    valid = w < total
    spare = jnp.maximum(w - total, 0)
    n_unused = jnp.maximum(n_blocks - n_used, 1)
    blk = jnp.where(valid, blk_start[e] + local % nbe, n_used + spare % n_unused)
    out_tile = jnp.where(valid, tile, spare // n_unused)
    flag = jnp.where(valid, jnp.where(local % nbe == 0, ITEM_NEW_WEIGHTS, ITEM_COMPUTE), ITEM_FILL)
    return (e, tile.astype(i32), blk.astype(i32), out_tile.astype(i32), flag.astype(i32))


def _combine_kernel(dest_ref, rows_ref, y_ref, gate_ref, x_ref, g2_ref, o_ref, buf_ref, sem):
    del rows_ref
    i = pl.program_id(0)
    tm = x_ref.shape[0]
    sub = y_ref.shape[1]

    def row_copy(p, slot):
        return pltpu.make_async_copy(y_ref.at[slot], buf_ref.at[pl.ds(pl.multiple_of(p * sub, sub), sub), :], sem)

    def issue(p, carry):
        k = p // tm
        t = p - k * tm
        row_copy(p, dest_ref[(i * tm + t) * TOP_K + k]).start()
        return carry

    def drain(p, carry):
        row_copy(p, 0).wait()
        return carry

    lax.fori_loop(0, TOP_K * tm, issue, 0)
    lax.fori_loop(0, TOP_K * tm, drain, 0)
    gate = gate_ref[...]
    for s in range(sub):
        cols = slice(s * LANES, (s + 1) * LANES)
        acc = gate[:, 0:1] * buf_ref[pl.ds(s, tm, stride=sub), :]
        for k in range(1, TOP_K):
            acc = acc + gate[:, k:k + 1] * buf_ref[pl.ds(k * tm * sub + s, tm, stride=sub), :]
        o_ref[:, cols] = x_ref[:, cols] + g2_ref[0, :, cols] * acc


def _combine(dest_flat, rows, y3, gates, x, mod, tm):
    t, d = x.shape
    sub = y3.shape[1]
    return pl.pallas_call(
        _combine_kernel,
        grid_spec=pltpu.PrefetchScalarGridSpec(
            num_scalar_prefetch=2, grid=(t // tm,),
            in_specs=[pl.BlockSpec(memory_space=pl.ANY),
                      pl.BlockSpec((tm, LANES), lambda i, dst, r: (i, 0)),
                      pl.BlockSpec((tm, d), lambda i, dst, r: (i, 0)),
                      pl.BlockSpec((1, 1, d), lambda i, dst, r: (r[i] * 6 + 5, 0, 0))],
            out_specs=pl.BlockSpec((tm, d), lambda i, dst, r: (i, 0)),
            scratch_shapes=[pltpu.VMEM((TOP_K * tm * sub, LANES), F32), pltpu.SemaphoreType.DMA(())]),
        out_shape=jax.ShapeDtypeStruct((t, d), F32),
        compiler_params=_cparams(("arbitrary",)),
        name="expert_combine",
    )(dest_flat, rows, y3, gates, x, mod)


def _final_kernel(x_ref, w_ref, o_ref):
    o_ref[...] = _rms(x_ref[...]) * w_ref[...]


def _final_norm(x, w, tm):
    t, d = x.shape
    return pl.pallas_call(
        _final_kernel,
        grid=(t // tm,),
        in_specs=[pl.BlockSpec((tm, d), lambda i: (i, 0)), pl.BlockSpec((1, d), lambda i: (0, 0))],
        out_specs=pl.BlockSpec((tm, d), lambda i: (i, 0)),
        out_shape=jax.ShapeDtypeStruct((t, d), F32),
        compiler_params=_cparams(("parallel",)),
        name="final_norm",
    )(x, w.reshape(1, d))


def _rope_tables(length):
    pairs = HEAD_DIM // 4
    t = np.arange(length)
    row = (t // GRID_W).astype(np.float32)
    col = (t % GRID_W).astype(np.float32)
    inv = jnp.asarray(ROPE_THETA, F32) ** (-jnp.arange(pairs, dtype=F32) / pairs)
    ang = jnp.concatenate([jnp.asarray(row)[:, None] * inv, jnp.asarray(col)[:, None] * inv], axis=-1)
    cos, sin = jnp.cos(ang), jnp.sin(ang)
    return jnp.concatenate([cos, cos], axis=-1), jnp.concatenate([-sin, sin], axis=-1)


def kernel(x_prompt, x_sample, c, cache_na_k, cache_na_v, cache_gqa_k, cache_gqa_v, c_ctx, w_ada, b_ada, w_in, q_norm, k_norm, na_rpb, conv_w, sgu_w, sgu_b, sgu_ln_g, sgu_ln_b, w_branch, w_gate, b_gate, w_out, w_router, b_router, w_gu, b_gu, w_down, b_down, final_norm):
    bp, lp, d = x_prompt.shape
    bs, ls, _ = x_sample.shape
    depth = w_ada.shape[0]
    past = cache_na_k.shape[2]
    tp, ts = bp * lp, bs * ls
    t = tp + ts
    bw = w_branch.shape[2]
    heads = bw // HEAD_DIM
    group = heads // N_KV_HEADS
    n_exp = w_router.shape[2]
    ff = w_down.shape[2]
    n_rows = ls // GRID_W
    assert bw % LANES == 0 and sgu_w.shape[1] * CHUNK == bw and sgu_w.shape[2] == CHUNK

    widths = (bw, bw, bw, heads * HEAD_DIM, N_KV_HEADS * HEAD_DIM, N_KV_HEADS * HEAD_DIM, bw, bw, bw, bw, bw)
    offs = np.concatenate([[0], np.cumsum(widths)])
    blk = [int(o) // LANES for o in offs]
    offs = [int(o) for o in offs]
    assert tp % ls == 0

    x = jnp.concatenate([x_prompt.reshape(tp, d), x_sample.reshape(ts, d)], axis=0)
    cv = jnp.concatenate([c_ctx[None, :], c, jnp.zeros((8 - 1 - bs, d), F32)], axis=0)
    mod_all = _ada_table(cv, w_ada, b_ada)
    cos_t, sin_t = _rope_tables(ls)
    cna_k = cache_na_k.reshape(bs, depth, past, heads * HEAD_DIM)
    cna_v = cache_na_v.reshape(bs, depth, past, heads * HEAD_DIM)
    cga_k = cache_gqa_k.reshape(bs, depth, past, N_KV_HEADS * HEAD_DIM)
    cga_v = cache_gqa_v.reshape(bs, depth, past, N_KV_HEADS * HEAD_DIM)

    tm_n = _pick(lp, (256, 128))
    tm_m = _pick(ls, (1024, 512, 256))
    tm_m = tm_m if tp % tm_m == 0 else tm_n
    rows_n = _tile_mod_rows(tm_n, tp, t, ls)
    rows_m = _tile_mod_rows(tm_m, tp, t, ls)
    tm_c = CHUNK
    rows_c = _tile_mod_rows(tm_c, tp, t, ls)
    tn = _pick(d, (512, 256, 128))
    tn_in = _pick(w_in.shape[2], (512, 256, 128))
    tf = _pick(ff, (512, 256, 128))
    tn_d = _pick(d, (2048, 1024, 512, 256, 128))
    n_assign = t * TOP_K
    n_blocks = -(-n_assign // MOE_BLOCK) + n_exp
    n_slots = n_blocks * MOE_BLOCK
    b_gu4 = b_gu.reshape(depth, n_exp, 1, 2 * ff)
    b_down4 = b_down.reshape(depth, n_exp, 1, d)

    st_na_k, st_na_v, st_ga_k, st_ga_v = [], [], [], []
    for l in range(depth):
        mod = mod_all[l].reshape(8 * 6, 1, d)
        h = _normmod(x, mod, rows_n, tm_n, 1, 0, BF16)
        proj = _matmul(h, w_in[l].astype(BF16), F32, tm_m, tn_in)

        y_na_p = _ctx_na_attention(proj, bp, lp, heads, blk[0], blk[1], blk[2])
        y_ga_p, ga_k_p = _ctx_gqa_attention(proj, q_norm[l], k_norm[l], bp, lp, heads, group, blk[3], blk[4], blk[5])
        st_na_k.append(proj[:tp, offs[1]:offs[2]].reshape(bp, lp, heads, HEAD_DIM))
        st_na_v.append(proj[:tp, offs[2]:offs[3]].reshape(bp, lp, heads, HEAD_DIM))
        st_ga_k.append(ga_k_p.reshape(bp, lp, N_KV_HEADS, HEAD_DIM))
        st_ga_v.append(proj[:tp, offs[5]:offs[6]].reshape(bp, lp, N_KV_HEADS, HEAD_DIM))

        bias = _na_bias_tables(na_rpb[l], n_rows)
        y_na_s = _lat_na_attention(proj, cna_k, cna_v, l, bias, tp, bs, ls, heads, blk[0], blk[1], blk[2])
        y_ga_s = _lat_gqa_attention(proj, cga_k, cga_v, l, cos_t, sin_t, q_norm[l], k_norm[l],
                                    tp, bs, ls, heads, group, blk[3], blk[4], blk[5])

        y_sc = jnp.concatenate([
            _short_conv(proj, conv_w[l], 0, bp, lp, bw, offs[6], offs[7], offs[8]),
            _short_conv(proj, conv_w[l], tp, bs, ls, bw, offs[6], offs[7], offs[8])], axis=0)
        y_sg = _spatial_gating(proj, sgu_ln_g[l], sgu_ln_b[l], sgu_w[l], sgu_b[l], bw, offs[9], offs[10])
        ys = jnp.stack([jnp.concatenate([y_na_p, y_na_s], axis=0), y_sc, y_sg,
                        jnp.concatenate([y_ga_p, y_ga_s], axis=0)], axis=0)

        merged = _gatebranch(h, w_gate[l].astype(BF16), b_gate[l], ys, w_branch[l].astype(BF16), tm_m, tn)
        x = _outproj(merged, w_out[l].astype(BF16), x, mod, rows_m, tm_m, tn)

        h2, top_idx, gates, rank, counts = _router(x, mod, rows_n, tm_n, w_router[l], b_router[l])
        counts = counts[0].astype(jnp.int32)
        n_blk_e = (counts + MOE_BLOCK - 1) // MOE_BLOCK
        blk_start = jnp.cumsum(n_blk_e) - n_blk_e
        dest = (blk_start[top_idx[:, :TOP_K]] * MOE_BLOCK + rank[:, :TOP_K]).reshape(-1)
        slot_tok = jnp.zeros((n_slots,), jnp.int32).at[dest].set(jnp.arange(n_assign, dtype=jnp.int32) // TOP_K)
        xs = _gather_rows(slot_tok, h2.reshape(t, d // LANES, LANES), n_slots)
        hs = _expert_up(_expert_schedule(n_blk_e, blk_start, ff // tf, n_blocks), xs, w_gu, b_gu4, l, tf)
        yb = _expert_down(_expert_schedule(n_blk_e, blk_start, d // tn_d, n_blocks), hs, w_down, b_down4, l, tn_d)
        x = _combine(dest, rows_c, yb.reshape(n_slots, d // LANES, LANES), gates, x, mod, tm_c)

    y = _final_norm(x, final_norm, tm_n)
    y_prompt = y[:tp].reshape(bp, lp, d)
    y_sample = y[tp:].reshape(bs, ls, d)
    return (y_prompt, y_sample, jnp.stack(st_na_k, axis=1), jnp.stack(st_na_v, axis=1),
            jnp.stack(st_ga_k, axis=1), jnp.stack(st_ga_v, axis=1))
```

```python
import functools

import numpy as np
import jax
import jax.numpy as jnp
from jax import lax
from jax.experimental import pallas as pl
from jax.experimental.pallas import tpu as pltpu

F32 = jnp.float32
BF16 = jnp.bfloat16

LANES = 128
HEAD_DIM = 128
GRID_W = 64
NA_WIN_R = 8
NA_WIN_C = 16
N_KV_HEADS = 2
ROPE_THETA = 10000.0
CHUNK = 128
TOP_K = 4
SWIGLU_ALPHA = 1.702
SWIGLU_LIMIT = 7.0
MOE_BLOCK = 256
NORM_EPS = 1e-6
NEG_INF = -1e30
ATT_SCALE = HEAD_DIM ** -0.5
NA_QROWS = 4
NA_KROWS = 12
VMEM_LIMIT = 56 * 1024 * 1024
SUBLANES = 8
ITEM_FILL, ITEM_COMPUTE, ITEM_NEW_WEIGHTS = 0, 1, 2


def _cparams(sem):
    return pltpu.CompilerParams(dimension_semantics=sem, vmem_limit_bytes=VMEM_LIMIT)


def _pick(n, prefs):
    for p in prefs:
        if n % p == 0:
            return p
    return n


def _tile_mod_rows(tm, n_prompt, n_total, seq_sample):
    starts = np.arange(0, n_total, tm)
    return jnp.asarray(np.where(starts < n_prompt, 0, 1 + (starts - n_prompt) // seq_sample), jnp.int32)


def _rms(x):
    return x * lax.rsqrt(jnp.mean(x * x, axis=-1, keepdims=True) + NORM_EPS)


def _store_token_major(ref, val):
    rows, width = val.shape
    sub = width // LANES
    for s in range(sub):
        ref[pl.ds(s, rows, stride=sub), :] = val[:, s * LANES:(s + 1) * LANES]


def _row_pitch(sub):
    return sub + SUBLANES if sub % (2 * SUBLANES) == 0 else sub


def _load_lane_chunk(buf, first_row, rows, pitch, s):
    return buf[pl.ds(first_row * pitch + s, rows, stride=pitch), :]


def _ada_kernel(cv_ref, w_ref, b_ref, o_ref):
    cv = cv_ref[...]
    a = (cv * jax.nn.sigmoid(cv)).astype(BF16)
    o_ref[0] = jnp.dot(a, w_ref[0].astype(BF16), preferred_element_type=F32) + b_ref[0]


def _ada_table(cv, w_ada, b_ada):
    depth, d, n = w_ada.shape
    tn = _pick(n, (512, 256, 128))
    return pl.pallas_call(
        _ada_kernel,
        grid=(depth, n // tn),
        in_specs=[pl.BlockSpec((8, d), lambda l, j: (0, 0)),
                  pl.BlockSpec((1, d, tn), lambda l, j: (l, 0, j)),
                  pl.BlockSpec((1, 1, tn), lambda l, j: (l, 0, j))],
        out_specs=pl.BlockSpec((1, 8, tn), lambda l, j: (l, 0, j)),
        out_shape=jax.ShapeDtypeStruct((depth, 8, n), F32),
        compiler_params=_cparams(("parallel", "parallel")),
        name="ada_table",
    )(cv, w_ada, b_ada.reshape(depth, 1, n))


def _normmod_kernel(rows_ref, x_ref, scale_ref, shift_ref, o_ref):
    del rows_ref
    o_ref[...] = (_rms(x_ref[...]) * (1.0 + scale_ref[0]) + shift_ref[0]).astype(o_ref.dtype)


def _normmod(x, mod, rows, tm, which_scale, which_shift, out_dtype):
    t, d = x.shape
    return pl.pallas_call(
        _normmod_kernel,
        grid_spec=pltpu.PrefetchScalarGridSpec(
            num_scalar_prefetch=1, grid=(t // tm,),
            in_specs=[pl.BlockSpec((tm, d), lambda i, r: (i, 0)),
                      pl.BlockSpec((1, 1, d), lambda i, r: (r[i] * 6 + which_scale, 0, 0)),
                      pl.BlockSpec((1, 1, d), lambda i, r: (r[i] * 6 + which_shift, 0, 0))],
            out_specs=pl.BlockSpec((tm, d), lambda i, r: (i, 0))),
        out_shape=jax.ShapeDtypeStruct((t, d), out_dtype),
        compiler_params=_cparams(("parallel",)),
        name="normmod",
    )(rows, x, mod, mod)


def _mm_kernel(a_ref, w_ref, o_ref):
    o_ref[...] = jnp.dot(a_ref[...], w_ref[...], preferred_element_type=F32).astype(o_ref.dtype)


def _matmul(a, w, out_dtype, tm, tn):
    m, k = a.shape
    n = w.shape[1]
    return pl.pallas_call(
        _mm_kernel,
        grid=(m // tm, n // tn),
        in_specs=[pl.BlockSpec((tm, k), lambda i, j: (i, 0)),
                  pl.BlockSpec((k, tn), lambda i, j: (0, j))],
        out_specs=pl.BlockSpec((tm, tn), lambda i, j: (i, j)),
        out_shape=jax.ShapeDtypeStruct((m, n), out_dtype),
        compiler_params=_cparams(("parallel", "parallel")),
        name="in_proj",
    )(a, w)


def _outproj_kernel(rows_ref, a_ref, w_ref, x_ref, g_ref, o_ref):
    del rows_ref
    o_ref[...] = x_ref[...] + g_ref[0] * jnp.dot(a_ref[...], w_ref[...], preferred_element_type=F32)


def _outproj(a, w, x, mod, rows, tm, tn):
    m, k = a.shape
    n = w.shape[1]
    return pl.pallas_call(
        _outproj_kernel,
        grid_spec=pltpu.PrefetchScalarGridSpec(
            num_scalar_prefetch=1, grid=(m // tm, n // tn),
            in_specs=[pl.BlockSpec((tm, k), lambda i, j, r: (i, 0)),
                      pl.BlockSpec((k, tn), lambda i, j, r: (0, j)),
                      pl.BlockSpec((tm, tn), lambda i, j, r: (i, j)),
                      pl.BlockSpec((1, 1, tn), lambda i, j, r: (r[i] * 6 + 2, 0, j))],
            out_specs=pl.BlockSpec((tm, tn), lambda i, j, r: (i, j))),
        out_shape=jax.ShapeDtypeStruct((m, n), F32),
        compiler_params=_cparams(("parallel", "parallel")),
        name="out_proj",
    )(rows, a, w, x, mod)


def _gatebranch_kernel(h_ref, wg_ref, bg_ref, y_ref, wb_ref, o_ref, acc_ref):
    br = pl.program_id(2)
    g = jnp.dot(h_ref[...], wg_ref[0], preferred_element_type=F32) + bg_ref[0]
    p = jnp.dot(y_ref[0], wb_ref[0], preferred_element_type=F32)
    v = jax.nn.sigmoid(g) * p

    @pl.when(br == 0)
    def _():
        acc_ref[...] = v

    @pl.when(br > 0)
    def _():
        acc_ref[...] += v

    @pl.when(br == pl.num_programs(2) - 1)
    def _():
        o_ref[...] = acc_ref[...].astype(o_ref.dtype)


def _gatebranch(h, w_gate, b_gate, ys, w_branch, tm, tn):
    t, d = h.shape
    nbr, bw, _ = w_branch.shape
    return pl.pallas_call(
        _gatebranch_kernel,
        grid=(t // tm, d // tn, nbr),
        in_specs=[pl.BlockSpec((tm, d), lambda i, j, b: (i, 0)),
                  pl.BlockSpec((1, d, tn), lambda i, j, b: (b, 0, j)),
                  pl.BlockSpec((1, 1, tn), lambda i, j, b: (b, 0, j)),
                  pl.BlockSpec((1, tm, bw), lambda i, j, b: (b, i, 0)),
                  pl.BlockSpec((1, bw, tn), lambda i, j, b: (b, 0, j))],
        out_specs=pl.BlockSpec((tm, tn), lambda i, j, b: (i, j)),
        out_shape=jax.ShapeDtypeStruct((t, d), BF16),
        scratch_shapes=[pltpu.VMEM((tm, tn), F32)],
        compiler_params=_cparams(("parallel", "parallel", "arbitrary")),
        name="gate_branch",
    )(h, w_gate, b_gate.reshape(nbr, 1, d), ys, w_branch)


def _softmax_parts(scores):
    m = scores[0].max(axis=-1, keepdims=True)
    for s in scores[1:]:
        m = jnp.maximum(m, s.max(axis=-1, keepdims=True))
    ps = [jnp.exp(s - m) for s in scores]
    l = ps[0].sum(axis=-1, keepdims=True)
    for p in ps[1:]:
        l = l + p.sum(axis=-1, keepdims=True)
    return ps, l


def _qk(q, k):
    return lax.dot_general(q, k, (((1,), (1,)), ((), ())), preferred_element_type=F32) * ATT_SCALE


def _ctx_na_kernel(q_ref, k_ref, v_ref, o_ref):
    s = _qk(q_ref[...].astype(BF16), k_ref[...].astype(BF16))
    (p,), l = _softmax_parts([s])
    o_ref[...] = (jnp.dot(p.astype(BF16), v_ref[...].astype(BF16), preferred_element_type=F32) / l).astype(o_ref.dtype)


def _ctx_na_attention(proj, nb, seq, heads, cq, ck, cv):
    return pl.pallas_call(
        _ctx_na_kernel,
        grid=(nb, heads),
        in_specs=[pl.BlockSpec((seq, HEAD_DIM), lambda b, h: (b, cq + h)),
                  pl.BlockSpec((seq, HEAD_DIM), lambda b, h: (b, ck + h)),
                  pl.BlockSpec((seq, HEAD_DIM), lambda b, h: (b, cv + h))],
        out_specs=pl.BlockSpec((seq, HEAD_DIM), lambda b, h: (b, h)),
        out_shape=jax.ShapeDtypeStruct((nb * seq, heads * HEAD_DIM), BF16),
        compiler_params=_cparams(("parallel", "parallel")),
        name="ctx_na_attn",
    )(proj, proj, proj)


def _ctx_gqa_kernel(q_ref, k_ref, v_ref, qn_ref, kn_ref, o_ref, ko_ref):
    kn = _rms(k_ref[...]) * kn_ref[...]
    ko_ref[...] = kn
    qn = _rms(q_ref[...]) * qn_ref[...]
    s = _qk(qn.astype(BF16), kn.astype(BF16))
    (p,), l = _softmax_parts([s])
    o_ref[...] = (jnp.dot(p.astype(BF16), v_ref[...].astype(BF16), preferred_element_type=F32) / l).astype(o_ref.dtype)


def _ctx_gqa_attention(proj, q_norm, k_norm, nb, seq, heads, group, cq, ck, cv):
    return pl.pallas_call(
        _ctx_gqa_kernel,
        grid=(nb, heads),
        in_specs=[pl.BlockSpec((seq, HEAD_DIM), lambda b, h: (b, cq + h)),
                  pl.BlockSpec((seq, HEAD_DIM), lambda b, h: (b, ck + h // group)),
                  pl.BlockSpec((seq, HEAD_DIM), lambda b, h: (b, cv + h // group)),
                  pl.BlockSpec((1, HEAD_DIM), lambda b, h: (0, 0)),
                  pl.BlockSpec((1, HEAD_DIM), lambda b, h: (0, 0))],
        out_specs=[pl.BlockSpec((seq, HEAD_DIM), lambda b, h: (b, h)),
                   pl.BlockSpec((seq, HEAD_DIM), lambda b, h: (b, h // group))],
        out_shape=[jax.ShapeDtypeStruct((nb * seq, heads * HEAD_DIM), BF16),
                   jax.ShapeDtypeStruct((nb * seq, (heads // group) * HEAD_DIM), F32)],
        compiler_params=_cparams(("parallel", "arbitrary")),
        name="ctx_gqa_attn",
    )(proj, proj, proj, q_norm.reshape(1, HEAD_DIM), k_norm.reshape(1, HEAD_DIM))


def _rope(x, cos_t, sin_t):
    return x * cos_t + pltpu.roll(x, HEAD_DIM // 2, 1) * sin_t


def _lat_gqa_kernel(q_ref, k_ref, v_ref, kc_ref, vc_ref, cos_ref, sin_ref, qn_ref, kn_ref, o_ref, kr_ref, *, tq):
    qi = pl.program_id(2)

    @pl.when(qi == 0)
    def _():
        kn = _rms(k_ref[...]) * kn_ref[...]
        kr_ref[...] = _rope(kn, cos_ref[...], sin_ref[...]).astype(BF16)

    rows = pl.ds(pl.multiple_of(qi * tq, tq), tq)
    qn = _rms(q_ref[...]) * qn_ref[...]
    qr = _rope(qn, cos_ref[rows, :], sin_ref[rows, :])
    s_ctx = _qk(qn.astype(BF16), kc_ref[0, 0].astype(BF16))
    s_lat = _qk(qr.astype(BF16), kr_ref[...])
    (p_ctx, p_lat), l = _softmax_parts([s_ctx, s_lat])
    out = (jnp.dot(p_ctx.astype(BF16), vc_ref[0, 0].astype(BF16), preferred_element_type=F32)
           + jnp.dot(p_lat.astype(BF16), v_ref[...].astype(BF16), preferred_element_type=F32))
    o_ref[...] = (out / l).astype(o_ref.dtype)


def _lat_gqa_attention(proj, cache_k, cache_v, layer, cos_t, sin_t, q_norm, k_norm,
                       row0, nb, seq, heads, group, cq, ck, cv):
    tq = _pick(seq, (512, 256, 128))
    nq = seq // tq
    rb = row0 // tq
    sb = row0 // seq
    past = cache_k.shape[2]
    return pl.pallas_call(
        functools.partial(_lat_gqa_kernel, tq=tq),
        grid=(nb, heads, nq),
        in_specs=[pl.BlockSpec((tq, HEAD_DIM), lambda b, h, i: (rb + b * nq + i, cq + h)),
                  pl.BlockSpec((seq, HEAD_DIM), lambda b, h, i: (sb + b, ck + h // group)),
                  pl.BlockSpec((seq, HEAD_DIM), lambda b, h, i: (sb + b, cv + h // group)),
                  pl.BlockSpec((1, 1, past, HEAD_DIM), lambda b, h, i: (b, layer, 0, h // group)),
                  pl.BlockSpec((1, 1, past, HEAD_DIM), lambda b, h, i: (b, layer, 0, h // group)),
                  pl.BlockSpec((seq, HEAD_DIM), lambda b, h, i: (0, 0)),
                  pl.BlockSpec((seq, HEAD_DIM), lambda b, h, i: (0, 0)),
                  pl.BlockSpec((1, HEAD_DIM), lambda b, h, i: (0, 0)),
                  pl.BlockSpec((1, HEAD_DIM), lambda b, h, i: (0, 0))],
        out_specs=pl.BlockSpec((tq, HEAD_DIM), lambda b, h, i: (b * nq + i, h)),
        out_shape=jax.ShapeDtypeStruct((nb * seq, heads * HEAD_DIM), BF16),
        scratch_shapes=[pltpu.VMEM((seq, HEAD_DIM), BF16)],
        compiler_params=_cparams(("parallel", "arbitrary", "arbitrary")),
        name="lat_gqa_attn",
    )(proj, proj, proj, cache_k, cache_v, cos_t, sin_t, q_norm.reshape(1, HEAD_DIM), k_norm.reshape(1, HEAD_DIM))


def _na_key_row0(i, n_rows):
    return jnp.clip(NA_QROWS * i - NA_WIN_R // 2, 0, n_rows - NA_KROWS)


def _lat_na_kernel(q_ref, k_ref, v_ref, kc_ref, vc_ref, bias_ref, o_ref, *, n_rows):
    i = pl.program_id(2)
    start = pl.multiple_of(_na_key_row0(i, n_rows) * GRID_W, GRID_W)
    keys = pl.ds(start, NA_KROWS * GRID_W)
    q = q_ref[...].astype(BF16)
    s_ctx = _qk(q, kc_ref[0, 0].astype(BF16))
    s_win = _qk(q, k_ref[keys, :].astype(BF16)) + bias_ref[0, 0, 0]
    (p_ctx, p_win), l = _softmax_parts([s_ctx, s_win])
    out = (jnp.dot(p_ctx.astype(BF16), vc_ref[0, 0].astype(BF16), preferred_element_type=F32)
           + jnp.dot(p_win.astype(BF16), v_ref[keys, :].astype(BF16), preferred_element_type=F32))
    o_ref[...] = (out / l).astype(o_ref.dtype)


def _na_bias_tables(rpb, n_rows):
    n_layers, heads = rpb.shape[:2]
    n_blocks = n_rows // NA_QROWS
    pad = GRID_W - NA_WIN_C
    padded = jnp.pad(rpb.astype(F32), ((0, 0), (0, 0), (0, 0), (pad, pad)))
    ccol = jnp.stack([padded[..., GRID_W - 1 - qc:2 * GRID_W - 1 - qc] for qc in range(GRID_W)], axis=-2)
    qc = np.arange(GRID_W)[:, None]
    kc = np.arange(GRID_W)[None, :]
    cs = np.clip(qc - NA_WIN_C // 2, 0, GRID_W - NA_WIN_C)
    ccol = jnp.where((kc >= cs) & (kc < cs + NA_WIN_C), ccol, NEG_INF)
    masked = jnp.full((n_layers, heads, GRID_W, GRID_W), NEG_INF, F32)
    kinds = []
    for i in (0, min(2, n_blocks - 1), n_blocks - 1):
        k0 = int(np.clip(NA_QROWS * i - NA_WIN_R // 2, 0, n_rows - NA_KROWS))
        q_rows = []
        for qr in range(NA_QROWS * i, NA_QROWS * (i + 1)):
            rs = int(np.clip(qr - NA_WIN_R // 2, 0, n_rows - NA_WIN_R))
            q_rows.append(jnp.concatenate(
                [ccol[:, :, kr - qr + NA_WIN_R - 1] if rs <= kr < rs + NA_WIN_R else masked
                 for kr in range(k0, k0 + NA_KROWS)], axis=-1))
        kinds.append(jnp.concatenate(q_rows, axis=-2))
    return jnp.stack(kinds, axis=2)


def _lat_na_attention(proj, cache_k, cache_v, layer, bias, row0, nb, seq, heads, cq, ck, cv):
    n_rows = seq // GRID_W
    tq = NA_QROWS * GRID_W
    nq = seq // tq
    rb = row0 // tq
    sb = row0 // seq
    past = cache_k.shape[2]
    kw = NA_KROWS * GRID_W

    def bias_map(b, h, i):
        return (layer, h, jnp.where(i == 0, 0, jnp.where(i == nq - 1, 2, 1)), 0, 0)

    return pl.pallas_call(
        functools.partial(_lat_na_kernel, n_rows=n_rows),
        grid=(nb, heads, nq),
        in_specs=[pl.BlockSpec((tq, HEAD_DIM), lambda b, h, i: (rb + b * nq + i, cq + h)),
                  pl.BlockSpec((seq, HEAD_DIM), lambda b, h, i: (sb + b, ck + h)),
                  pl.BlockSpec((seq, HEAD_DIM), lambda b, h, i: (sb + b, cv + h)),
                  pl.BlockSpec((1, 1, past, HEAD_DIM), lambda b, h, i: (b, layer, 0, h)),
                  pl.BlockSpec((1, 1, past, HEAD_DIM), lambda b, h, i: (b, layer, 0, h)),
                  pl.BlockSpec((1, 1, 1, tq, kw), bias_map)],
        out_specs=pl.BlockSpec((tq, HEAD_DIM), lambda b, h, i: (b * nq + i, h)),
        out_shape=jax.ShapeDtypeStruct((nb * seq, heads * HEAD_DIM), BF16),
        compiler_params=_cparams(("parallel", "parallel", "arbitrary")),
        name="lat_na_attn",
    )(proj, proj, proj, cache_k, cache_v, bias)


def _conv_kernel(b_ref, c_ref, x_ref, w_ref, o_ref):
    p = c_ref[...] * x_ref[...]
    n = p.shape[0]
    row = lax.broadcasted_iota(jnp.int32, p.shape, 0)
    prev = jnp.where(row == 0, 0.0, pltpu.roll(p, 1, 0))
    nxt = jnp.where(row == n - 1, 0.0, pltpu.roll(p, n - 1, 0))
    w = w_ref[...]
    o_ref[...] = (b_ref[...] * (prev * w[0:1] + p * w[1:2] + nxt * w[2:3])).astype(o_ref.dtype)


def _short_conv(proj, conv_w, row0, nb, seq, width, off_b, off_c, off_x):
    tc = _pick(width, (256, 128))
    assert off_b % tc == 0 and off_c % tc == 0 and off_x % tc == 0
    sb = row0 // seq
    k = width // tc
    cb, cc, cx = off_b // tc, off_c // tc, off_x // tc
    return pl.pallas_call(
        _conv_kernel,
        grid=(nb, k),
        in_specs=[pl.BlockSpec((seq, tc), lambda b, j: (sb + b, cb + j)),
                  pl.BlockSpec((seq, tc), lambda b, j: (sb + b, cc + j)),
                  pl.BlockSpec((seq, tc), lambda b, j: (sb + b, cx + j)),
                  pl.BlockSpec((3, tc), lambda b, j: (0, j))],
        out_specs=pl.BlockSpec((seq, tc), lambda b, j: (b, j)),
        out_shape=jax.ShapeDtypeStruct((nb * seq, width), BF16),
        compiler_params=_cparams(("parallel", "parallel")),
        name="short_conv",
    )(proj, proj, proj, conv_w)


def _sgu_kernel(*refs, pieces):
    u_refs, v_refs = refs[:pieces], refs[pieces:2 * pieces]
    g_ref, b_ref, w_ref, bt_ref, o_ref = refs[2 * pieces:]
    rows, width = o_ref.shape
    groups = w_ref.shape[0]
    gw = width // groups
    pw = width // pieces
    for c in range(rows // CHUNK):
        rs = slice(c * CHUNK, (c + 1) * CHUNK)
        v = jnp.concatenate([r[rs, :] for r in v_refs], axis=-1)
        mu = jnp.mean(v, axis=-1, keepdims=True)
        var = jnp.mean(jnp.square(v - mu), axis=-1, keepdims=True)
        vn = ((v - mu) * lax.rsqrt(var + NORM_EPS)) * g_ref[...] + b_ref[...]
        for g in range(groups):
            cs = slice(g * gw, (g + 1) * gw)
            mixed = jnp.dot(w_ref[g].astype(BF16), vn[:, cs].astype(BF16), preferred_element_type=F32)
            mixed = mixed + bt_ref[:, g:g + 1]
            piece, lo = divmod(g * gw, pw)
            o_ref[rs, cs] = (u_refs[piece][rs, lo:lo + gw] * mixed).astype(o_ref.dtype)


def _spatial_gating(proj, ln_g, ln_b, w_s, b_s, width, off_u, off_v):
    t = proj.shape[0]
    rows = _pick(t, (256, 128))
    groups = w_s.shape[0]
    pw = int(np.gcd.reduce([width, off_u, off_v]))
    pieces = width // pw
    assert pw % (width // groups) == 0
    col_specs = [pl.BlockSpec((rows, pw), functools.partial(lambda i, c: (i, c), c=off // pw + p))
                 for off in (off_u, off_v) for p in range(pieces)]
    return pl.pallas_call(
        functools.partial(_sgu_kernel, pieces=pieces),
        grid=(t // rows,),
        in_specs=col_specs + [pl.BlockSpec((1, width), lambda i: (0, 0)),
                              pl.BlockSpec((1, width), lambda i: (0, 0)),
                              pl.BlockSpec((groups, CHUNK, CHUNK), lambda i: (0, 0, 0)),
                              pl.BlockSpec((CHUNK, groups), lambda i: (0, 0))],
        out_specs=pl.BlockSpec((rows, width), lambda i: (i, 0)),
        out_shape=jax.ShapeDtypeStruct((t, width), BF16),
        compiler_params=_cparams(("parallel",)),
        name="spatial_gating",
    )(*([proj] * (2 * pieces)), ln_g.reshape(1, width), ln_b.reshape(1, width), w_s, b_s.T)


def _router_kernel(rows_ref, x_ref, scale_ref, shift_ref, wr_ref, br_ref,
                   h_ref, idx_ref, gate_ref, rank_ref, cnt_ref, carry_ref):
    del rows_ref
    i = pl.program_id(0)

    @pl.when(i == 0)
    def _():
        carry_ref[...] = jnp.zeros_like(carry_ref)

    h = _rms(x_ref[...]) * (1.0 + scale_ref[0]) + shift_ref[0]
    _store_token_major(h_ref, h)
    logits = jnp.dot(h, wr_ref[...], preferred_element_type=F32, precision=lax.Precision.HIGHEST) + br_ref[...]
    tm, n_exp = logits.shape
    lane_e = lax.broadcasted_iota(jnp.int32, (tm, n_exp), 1)
    lane_o = lax.broadcasted_iota(jnp.int32, (tm, LANES), 1)

    work = logits
    vals, sels = [], []
    for _ in range(TOP_K):
        m = work.max(axis=-1, keepdims=True)
        sel = jnp.min(jnp.where(work == m, lane_e, n_exp), axis=-1, keepdims=True)
        vals.append(m)
        sels.append(sel)
        work = jnp.where(lane_e == sel, -jnp.inf, work)
    exps = [jnp.exp(v - vals[0]) for v in vals]
    denom = exps[0]
    for e in exps[1:]:
        denom = denom + e

    onehot = jnp.zeros((tm, n_exp), F32)
    for sel in sels:
        onehot = onehot + (lane_e == sel).astype(F32)
    r = lax.broadcasted_iota(jnp.int32, (tm, tm), 0)
    c = lax.broadcasted_iota(jnp.int32, (tm, tm), 1)
    before = (r > c).astype(BF16)
    prefix = jnp.dot(before, onehot.astype(BF16), preferred_element_type=F32) + carry_ref[...]
    carry = carry_ref[...] + onehot.sum(axis=0, keepdims=True)
    carry_ref[...] = carry
    cnt_ref[...] = carry

    idx_out = jnp.zeros((tm, LANES), jnp.int32)
    gate_out = jnp.zeros((tm, LANES), F32)
    rank_out = jnp.zeros((tm, LANES), jnp.int32)
    for k in range(TOP_K):
        rank = jnp.sum(jnp.where(lane_e == sels[k], prefix, 0.0), axis=-1, keepdims=True).astype(jnp.int32)
        idx_out = jnp.where(lane_o == k, sels[k], idx_out)
        gate_out = jnp.where(lane_o == k, exps[k] / denom, gate_out)
        rank_out = jnp.where(lane_o == k, rank, rank_out)
    idx_ref[...] = idx_out
    gate_ref[...] = gate_out
    rank_ref[...] = rank_out


def _router(x, mod, rows, tm, w_router, b_router):
    t, d = x.shape
    n_exp = w_router.shape[1]
    return pl.pallas_call(
        _router_kernel,
        grid_spec=pltpu.PrefetchScalarGridSpec(
            num_scalar_prefetch=1, grid=(t // tm,),
            in_specs=[pl.BlockSpec((tm, d), lambda i, r: (i, 0)),
                      pl.BlockSpec((1, 1, d), lambda i, r: (r[i] * 6 + 4, 0, 0)),
                      pl.BlockSpec((1, 1, d), lambda i, r: (r[i] * 6 + 3, 0, 0)),
                      pl.BlockSpec((d, n_exp), lambda i, r: (0, 0)),
                      pl.BlockSpec((1, n_exp), lambda i, r: (0, 0))],
            out_specs=[pl.BlockSpec((tm * d // LANES, LANES), lambda i, r: (i, 0)),
                       pl.BlockSpec((tm, LANES), lambda i, r: (i, 0)),
                       pl.BlockSpec((tm, LANES), lambda i, r: (i, 0)),
                       pl.BlockSpec((tm, LANES), lambda i, r: (i, 0)),
                       pl.BlockSpec((1, n_exp), lambda i, r: (0, 0))],
            scratch_shapes=[pltpu.VMEM((1, n_exp), F32)]),
        out_shape=[jax.ShapeDtypeStruct((t * d // LANES, LANES), F32),
                   jax.ShapeDtypeStruct((t, LANES), jnp.int32),
                   jax.ShapeDtypeStruct((t, LANES), F32),
                   jax.ShapeDtypeStruct((t, LANES), jnp.int32),
                   jax.ShapeDtypeStruct((1, n_exp), F32)],
        compiler_params=_cparams(("arbitrary",)),
        name="router",
    )(rows, x, mod, mod, w_router, b_router.reshape(1, n_exp))


def _wait_rows(buf_ref, n_rows, sub, sem):
    done = buf_ref.at[pl.ds(0, n_rows * sub), :]
    pltpu.make_async_copy(done, done, sem).wait()


def _gather_kernel(tok_ref, src_ref, o_ref, buf0, buf1, sems):
    i = pl.program_id(0)
    n_rows = o_ref.shape[0]
    sub = src_ref.shape[1]
    pitch = _row_pitch(sub)
    bufs = (buf0, buf1)

    def issue(block, slot):
        def body(r, carry):
            rows = pl.ds(pl.multiple_of(r * pitch, SUBLANES), sub)
            pltpu.make_async_copy(src_ref.at[tok_ref[block * n_rows + r]], bufs[slot].at[rows, :], sems.at[slot]).start()
            return carry
        lax.fori_loop(0, n_rows, body, 0, unroll=8)

    for slot in (0, 1):
        @pl.when(i % 2 == slot)
        def _(slot=slot):
            if slot == 0:
                @pl.when(i == 0)
                def _():
                    issue(0, 0)

            @pl.when(i + 1 < pl.num_programs(0))
            def _():
                issue(i + 1, 1 - slot)

            _wait_rows(bufs[slot], n_rows, sub, sems.at[slot])
            for s in range(sub):
                o_ref[:, s * LANES:(s + 1) * LANES] = _load_lane_chunk(bufs[slot], 0, n_rows, pitch, s).astype(o_ref.dtype)


def _gather_rows(slot_tok, src3, n_slots):
    _, sub, _ = src3.shape
    d = sub * LANES
    buf = pltpu.VMEM((MOE_BLOCK * _row_pitch(sub), LANES), F32)
    return pl.pallas_call(
        _gather_kernel,
        grid_spec=pltpu.PrefetchScalarGridSpec(
            num_scalar_prefetch=1, grid=(n_slots // MOE_BLOCK,),
            in_specs=[pl.BlockSpec(memory_space=pl.ANY)],
            out_specs=pl.BlockSpec((MOE_BLOCK, d), lambda i, t: (i, 0)),
            scratch_shapes=[buf, buf, pltpu.SemaphoreType.DMA((2,))]),
        out_shape=jax.ShapeDtypeStruct((n_slots, d), BF16),
        compiler_params=_cparams(("arbitrary",)),
        name="expert_gather",
    )(slot_tok, src3)


def _stream_weights(sched_refs, tile_copies, n_parts, wbuf, wbf_refs):
    e_ref, t_ref, flag_ref, ne_ref, nt_ref, slot_ref = sched_refs
    w = pl.program_id(0)

    @pl.when(flag_ref[w] == ITEM_NEW_WEIGHTS)
    def _():
        slot = slot_ref[w]

        @pl.when(w == 0)
        def _():
            for c in tile_copies(e_ref[0], t_ref[0], slot):
                c.start()

        @pl.when(ne_ref[w] >= 0)
        def _():
            for c in tile_copies(ne_ref[w], nt_ref[w], 1 - slot):
                c.start()

        for c in tile_copies(e_ref[w], t_ref[w], slot):
            c.wait()
        for p in range(n_parts):
            wbf_refs[p][...] = wbuf[slot, p].astype(BF16)


def _expert_up_kernel(we_ref, wt_ref, wb_ref, wo_ref, flag_ref, ne_ref, nt_ref, slot_ref,
                      x_ref, w_hbm, bg_ref, bu_ref, o_ref, wbuf, wg_bf, wu_bf, sems, *, layer, tf, nf):
    del wb_ref, wo_ref
    flag = flag_ref[pl.program_id(0)]

    def tile_copies(e, t, slot):
        cols_g = pl.ds(pl.multiple_of(t * tf, tf), tf)
        cols_u = pl.ds(pl.multiple_of((nf + t) * tf, tf), tf)
        return (pltpu.make_async_copy(w_hbm.at[layer, e, :, cols_g], wbuf.at[slot, 0], sems.at[slot, 0]),
                pltpu.make_async_copy(w_hbm.at[layer, e, :, cols_u], wbuf.at[slot, 1], sems.at[slot, 1]))

    _stream_weights((we_ref, wt_ref, flag_ref, ne_ref, nt_ref, slot_ref), tile_copies, 2, wbuf, (wg_bf, wu_bf))

    @pl.when(flag != ITEM_FILL)
    def _():
        x = x_ref[...]
        g = jnp.dot(x, wg_bf[...], preferred_element_type=F32) + bg_ref[0, 0]
        u = jnp.dot(x, wu_bf[...], preferred_element_type=F32) + bu_ref[0, 0]
        g = jnp.minimum(g, SWIGLU_LIMIT)
        u = jnp.clip(u, -SWIGLU_LIMIT, SWIGLU_LIMIT)
        o_ref[...] = (g * jax.nn.sigmoid(SWIGLU_ALPHA * g) * (u + 1.0)).astype(o_ref.dtype)

    @pl.when(flag == ITEM_FILL)
    def _():
        o_ref[...] = jnp.zeros_like(o_ref)


def _expert_up(sched, xs, w_gu, b_gu4, layer, tf):
    n_slots, d = xs.shape
    ff = w_gu.shape[3] // 2
    nf = ff // tf
    n_items = sched[0].shape[0]
    return pl.pallas_call(
        functools.partial(_expert_up_kernel, layer=layer, tf=tf, nf=nf),
        grid_spec=pltpu.PrefetchScalarGridSpec(
            num_scalar_prefetch=8, grid=(n_items,),
            in_specs=[pl.BlockSpec((MOE_BLOCK, d), lambda w, e, t, b, *_: (b[w], 0)),
                      pl.BlockSpec(memory_space=pl.ANY),
                      pl.BlockSpec((1, 1, 1, tf), lambda w, e, t, *_: (layer, e[w], 0, t[w])),
                      pl.BlockSpec((1, 1, 1, tf), lambda w, e, t, *_: (layer, e[w], 0, nf + t[w]))],
            out_specs=pl.BlockSpec((MOE_BLOCK, tf), lambda w, e, t, b, o, *_: (b[w], o[w])),
            scratch_shapes=[pltpu.VMEM((2, 2, d, tf), F32), pltpu.VMEM((d, tf), BF16), pltpu.VMEM((d, tf), BF16),
                            pltpu.SemaphoreType.DMA((2, 2))]),
        out_shape=jax.ShapeDtypeStruct((n_slots, ff), BF16),
        compiler_params=_cparams(("arbitrary",)),
        name="expert_up",
    )(*sched, xs, w_gu, b_gu4, b_gu4)


def _expert_down_kernel(we_ref, wt_ref, wb_ref, wo_ref, flag_ref, ne_ref, nt_ref, slot_ref,
                        h_ref, w_hbm, bd_ref, o_ref, wbuf, wd_bf, sems, *, layer, tn):
    del wb_ref, wo_ref
    flag = flag_ref[pl.program_id(0)]

    def tile_copies(e, t, slot):
        cols = pl.ds(pl.multiple_of(t * tn, tn), tn)
        return (pltpu.make_async_copy(w_hbm.at[layer, e, :, cols], wbuf.at[slot, 0], sems.at[slot, 0]),)

    _stream_weights((we_ref, wt_ref, flag_ref, ne_ref, nt_ref, slot_ref), tile_copies, 1, wbuf, (wd_bf,))

    @pl.when(flag != ITEM_FILL)
    def _():
        _store_token_major(o_ref, jnp.dot(h_ref[...], wd_bf[...], preferred_element_type=F32) + bd_ref[0, 0])

    @pl.when(flag == ITEM_FILL)
    def _():
        o_ref[...] = jnp.zeros_like(o_ref)


def _expert_down(sched, hs, w_down, b_down4, layer, tn):
    n_slots, ff = hs.shape
    d = w_down.shape[3]
    n_items = sched[0].shape[0]
    n_blocks = n_slots // MOE_BLOCK
    planes, sub = d // tn, tn // LANES
    out = pl.pallas_call(
        functools.partial(_expert_down_kernel, layer=layer, tn=tn),
        grid_spec=pltpu.PrefetchScalarGridSpec(
            num_scalar_prefetch=8, grid=(n_items,),
            in_specs=[pl.BlockSpec((MOE_BLOCK, ff), lambda w, e, t, b, *_: (b[w], 0)),
                      pl.BlockSpec(memory_space=pl.ANY),
                      pl.BlockSpec((1, 1, 1, tn), lambda w, e, t, *_: (layer, e[w], 0, t[w]))],
            out_specs=pl.BlockSpec((MOE_BLOCK * sub, LANES), lambda w, e, t, b, o, *_: (o[w] * n_blocks + b[w], 0)),
            scratch_shapes=[pltpu.VMEM((2, 1, ff, tn), F32), pltpu.VMEM((ff, tn), BF16),
                            pltpu.SemaphoreType.DMA((2, 1))]),
        out_shape=jax.ShapeDtypeStruct((planes * n_slots * sub, LANES), F32),
        compiler_params=_cparams(("arbitrary",)),
        name="expert_down",
    )(*sched, hs, w_down, b_down4)
    return out.reshape(planes, n_slots, sub, LANES)


def _expert_schedule(n_blk_e, blk_start, n_tiles, n_blocks):
    i32 = jnp.int32
    n_exp = n_blk_e.shape[0]
    items = n_blk_e * n_tiles
    item_end = jnp.cumsum(items)
    n_used = jnp.sum(n_blk_e)
    total = n_used * n_tiles
    w = jnp.arange(n_tiles * n_blocks, dtype=i32)
    wc = jnp.minimum(w, total - 1)
    e = jnp.minimum(jnp.sum(item_end[None, :] <= wc[:, None], axis=1), n_exp - 1).astype(i32)
    of_e = e[:, None] == jnp.arange(n_exp, dtype=i32)[None, :]

    def lookup(table):
        return jnp.sum(jnp.where(of_e, table[None, :], 0), axis=1)

    local = wc - lookup(item_end - items)
    nbe = jnp.maximum(lookup(n_blk_e), 1)
    tile = local // nbe
    blk_start = lookup(blk_start)
    valid = w < total
    spare = jnp.maximum(w - total, 0)
    n_unused = jnp.maximum(n_blocks - n_used, 1)
    blk = jnp.where(valid, blk_start + local % nbe, n_used + spare % n_unused)
    out_tile = jnp.where(valid, tile, spare // n_unused)
    flag = jnp.where(valid, jnp.where(local % nbe == 0, ITEM_NEW_WEIGHTS, ITEM_COMPUTE), ITEM_FILL)
    experts = jnp.arange(n_exp, dtype=i32)
    nonempty = n_blk_e > 0
    before = jnp.cumsum(nonempty) - nonempty
    later = nonempty[None, :] & (experts[None, :] > experts[:, None])
    next_nonempty = jnp.min(jnp.where(later, experts[None, :], n_exp), axis=1)
    group = lookup(before) * n_tiles + tile
    last_tile = tile + 1 >= n_tiles
    next_e = jnp.where(last_tile, lookup(next_nonempty), e)
    next_e = jnp.where(group + 1 < jnp.sum(nonempty) * n_tiles, next_e, -1)
    next_tile = jnp.where(last_tile, 0, tile + 1)
    return tuple(a.astype(i32) for a in (e, tile, blk, out_tile, flag, next_e, next_tile, group % 2))


def _combine_kernel(dest_ref, rows_ref, y_ref, gate_ref, x_ref, g2_ref, o_ref, *scratch):
    del rows_ref
    planes = y_ref.shape[0]
    bufs, sems = scratch[:-1], scratch[-1]
    i = pl.program_id(0)
    tm = x_ref.shape[0]
    sub = y_ref.shape[2]
    pitch = _row_pitch(sub)

    def issue(tile, slot):
        def body(p, carry):
            k = p // tm
            t = p - k * tm
            row = dest_ref[(tile * tm + t) * TOP_K + k]
            rows = pl.ds(pl.multiple_of(p * pitch, SUBLANES), sub)
            for j in range(planes):
                b = slot * planes + j
                pltpu.make_async_copy(y_ref.at[j, row], bufs[b].at[rows, :], sems.at[b]).start()
            return carry
        lax.fori_loop(0, TOP_K * tm, body, 0, unroll=8)

    for slot in (0, 1):
        @pl.when(i % 2 == slot)
        def _(slot=slot):
            if slot == 0:
                @pl.when(i == 0)
                def _():
                    issue(0, 0)

            @pl.when(i + 1 < pl.num_programs(0))
            def _():
                issue(i + 1, 1 - slot)

            gate = gate_ref[...]
            for j in range(planes):
                buf = bufs[slot * planes + j]
                _wait_rows(buf, TOP_K * tm, sub, sems.at[slot * planes + j])
                for s in range(sub):
                    c0 = (j * sub + s) * LANES
                    acc = gate[:, 0:1] * _load_lane_chunk(buf, 0, tm, pitch, s)
                    for k in range(1, TOP_K):
                        acc = acc + gate[:, k:k + 1] * _load_lane_chunk(buf, k * tm, tm, pitch, s)
                    o_ref[:, c0:c0 + LANES] = x_ref[:, c0:c0 + LANES] + g2_ref[0, :, c0:c0 + LANES] * acc


def _combine(dest_flat, rows, y4, gates, x, mod, tm):
    t, d = x.shape
    planes, _, sub, _ = y4.shape
    bufs = [pltpu.VMEM((TOP_K * tm * _row_pitch(sub), LANES), F32) for _ in range(2 * planes)]
    return pl.pallas_call(
        _combine_kernel,
        grid_spec=pltpu.PrefetchScalarGridSpec(
            num_scalar_prefetch=2, grid=(t // tm,),
            in_specs=[pl.BlockSpec(memory_space=pl.ANY),
                      pl.BlockSpec((tm, LANES), lambda i, dst, r: (i, 0)),
                      pl.BlockSpec((tm, d), lambda i, dst, r: (i, 0)),
                      pl.BlockSpec((1, 1, d), lambda i, dst, r: (r[i] * 6 + 5, 0, 0))],
            out_specs=pl.BlockSpec((tm, d), lambda i, dst, r: (i, 0)),
            scratch_shapes=bufs + [pltpu.SemaphoreType.DMA((2 * planes,))]),
        out_shape=jax.ShapeDtypeStruct((t, d), F32),
        compiler_params=_cparams(("arbitrary",)),
        name="expert_combine",
    )(dest_flat, rows, y4, gates, x, mod)


def _final_kernel(x_ref, w_ref, o_ref):
    o_ref[...] = _rms(x_ref[...]) * w_ref[...]


def _final_norm(x, w, tm):
    t, d = x.shape
    return pl.pallas_call(
        _final_kernel,
        grid=(t // tm,),
        in_specs=[pl.BlockSpec((tm, d), lambda i: (i, 0)), pl.BlockSpec((1, d), lambda i: (0, 0))],
        out_specs=pl.BlockSpec((tm, d), lambda i: (i, 0)),
        out_shape=jax.ShapeDtypeStruct((t, d), F32),
        compiler_params=_cparams(("parallel",)),
        name="final_norm",
    )(x, w.reshape(1, d))


def _rope_tables(length):
    pairs = HEAD_DIM // 4
    t = np.arange(length)
    row = (t // GRID_W).astype(np.float32)
    col = (t % GRID_W).astype(np.float32)
    inv = jnp.asarray(ROPE_THETA, F32) ** (-jnp.arange(pairs, dtype=F32) / pairs)
    ang = jnp.concatenate([jnp.asarray(row)[:, None] * inv, jnp.asarray(col)[:, None] * inv], axis=-1)
    cos, sin = jnp.cos(ang), jnp.sin(ang)
    return jnp.concatenate([cos, cos], axis=-1), jnp.concatenate([-sin, sin], axis=-1)


def kernel(x_prompt, x_sample, c, cache_na_k, cache_na_v, cache_gqa_k, cache_gqa_v, c_ctx, w_ada, b_ada, w_in, q_norm, k_norm, na_rpb, conv_w, sgu_w, sgu_b, sgu_ln_g, sgu_ln_b, w_branch, w_gate, b_gate, w_out, w_router, b_router, w_gu, b_gu, w_down, b_down, final_norm):
    bp, lp, d = x_prompt.shape
    bs, ls, _ = x_sample.shape
    depth = w_ada.shape[0]
    past = cache_na_k.shape[2]
    tp, ts = bp * lp, bs * ls
    t = tp + ts
    bw = w_branch.shape[2]
    heads = bw // HEAD_DIM
    group = heads // N_KV_HEADS
    n_exp = w_router.shape[2]
    ff = w_down.shape[2]
    n_rows = ls // GRID_W
    assert bw % LANES == 0 and sgu_w.shape[1] * CHUNK == bw and sgu_w.shape[2] == CHUNK

    widths = (bw, bw, bw, heads * HEAD_DIM, N_KV_HEADS * HEAD_DIM, N_KV_HEADS * HEAD_DIM, bw, bw, bw, bw, bw)
    offs = np.concatenate([[0], np.cumsum(widths)])
    blk = [int(o) // LANES for o in offs]
    offs = [int(o) for o in offs]
    assert tp % ls == 0

    x = jnp.concatenate([x_prompt.reshape(tp, d), x_sample.reshape(ts, d)], axis=0)
    cv = jnp.concatenate([c_ctx[None, :], c, jnp.zeros((8 - 1 - bs, d), F32)], axis=0)
    mod_all = _ada_table(cv, w_ada, b_ada)
    cos_t, sin_t = _rope_tables(ls)
    bias = _na_bias_tables(na_rpb, n_rows)
    cna_k = cache_na_k.reshape(bs, depth, past, heads * HEAD_DIM)
    cna_v = cache_na_v.reshape(bs, depth, past, heads * HEAD_DIM)
    cga_k = cache_gqa_k.reshape(bs, depth, past, N_KV_HEADS * HEAD_DIM)
    cga_v = cache_gqa_v.reshape(bs, depth, past, N_KV_HEADS * HEAD_DIM)

    tm_n = _pick(lp, (256, 128))
    tm_m = _pick(ls, (1024, 512, 256))
    tm_m = tm_m if tp % tm_m == 0 else tm_n
    rows_n = _tile_mod_rows(tm_n, tp, t, ls)
    rows_m = _tile_mod_rows(tm_m, tp, t, ls)
    tm_c = CHUNK
    rows_c = _tile_mod_rows(tm_c, tp, t, ls)
    tn = _pick(d, (512, 256, 128))
    tn_in = _pick(w_in.shape[2], (512, 256, 128))
    tf = _pick(ff, (512, 256, 128))
    tn_d = _pick(d, (2048, 1024, 512, 256, 128))
    n_assign = t * TOP_K
    n_blocks = -(-n_assign // MOE_BLOCK) + n_exp
    n_slots = n_blocks * MOE_BLOCK
    b_gu4 = b_gu.reshape(depth, n_exp, 1, 2 * ff)
    b_down4 = b_down.reshape(depth, n_exp, 1, d)

    st_na_k, st_na_v, st_ga_k, st_ga_v = [], [], [], []
    for l in range(depth):
        mod = mod_all[l].reshape(8 * 6, 1, d)
        h = _normmod(x, mod, rows_n, tm_n, 1, 0, BF16)
        proj = _matmul(h, w_in[l].astype(BF16), F32, tm_m, tn_in)

        y_na_p = _ctx_na_attention(proj, bp, lp, heads, blk[0], blk[1], blk[2])
        y_ga_p, ga_k_p = _ctx_gqa_attention(proj, q_norm[l], k_norm[l], bp, lp, heads, group, blk[3], blk[4], blk[5])
        st_na_k.append(proj[:tp, offs[1]:offs[2]].reshape(bp, lp, heads, HEAD_DIM))
        st_na_v.append(proj[:tp, offs[2]:offs[3]].reshape(bp, lp, heads, HEAD_DIM))
        st_ga_k.append(ga_k_p.reshape(bp, lp, N_KV_HEADS, HEAD_DIM))
        st_ga_v.append(proj[:tp, offs[5]:offs[6]].reshape(bp, lp, N_KV_HEADS, HEAD_DIM))

        y_na_s = _lat_na_attention(proj, cna_k, cna_v, l, bias, tp, bs, ls, heads, blk[0], blk[1], blk[2])
        y_ga_s = _lat_gqa_attention(proj, cga_k, cga_v, l, cos_t, sin_t, q_norm[l], k_norm[l],
                                    tp, bs, ls, heads, group, blk[3], blk[4], blk[5])

        y_sc = jnp.concatenate([
            _short_conv(proj, conv_w[l], 0, bp, lp, bw, offs[6], offs[7], offs[8]),
            _short_conv(proj, conv_w[l], tp, bs, ls, bw, offs[6], offs[7], offs[8])], axis=0)
        y_sg = _spatial_gating(proj, sgu_ln_g[l], sgu_ln_b[l], sgu_w[l], sgu_b[l], bw, offs[9], offs[10])
        ys = jnp.stack([jnp.concatenate([y_na_p, y_na_s], axis=0), y_sc, y_sg,
                        jnp.concatenate([y_ga_p, y_ga_s], axis=0)], axis=0)

        merged = _gatebranch(h, w_gate[l].astype(BF16), b_gate[l], ys, w_branch[l].astype(BF16), tm_m, tn)
        x = _outproj(merged, w_out[l].astype(BF16), x, mod, rows_m, tm_m, tn)

        h2, top_idx, gates, rank, counts = _router(x, mod, rows_n, tm_n, w_router[l], b_router[l])
        counts = counts[0].astype(jnp.int32)
        n_blk_e = (counts + MOE_BLOCK - 1) // MOE_BLOCK
        blk_start = jnp.cumsum(n_blk_e) - n_blk_e
        of_e = top_idx[:, :TOP_K, None] == jnp.arange(n_exp, dtype=jnp.int32)
        dest = (jnp.sum(jnp.where(of_e, blk_start, 0), axis=-1) * MOE_BLOCK + rank[:, :TOP_K]).reshape(-1)
        slot_tok = jnp.zeros((n_slots,), jnp.int32).at[dest].set(jnp.arange(n_assign, dtype=jnp.int32) // TOP_K)
        xs = _gather_rows(slot_tok, h2.reshape(t, d // LANES, LANES), n_slots)
        hs = _expert_up(_expert_schedule(n_blk_e, blk_start, ff // tf, n_blocks), xs, w_gu, b_gu4, l, tf)
        yb = _expert_down(_expert_schedule(n_blk_e, blk_start, d // tn_d, n_blocks), hs, w_down, b_down4, l, tn_d)
        x = _combine(dest, rows_c, yb, gates, x, mod, tm_c)

    y = _final_norm(x, final_norm, tm_n)
    y_prompt = y[:tp].reshape(bp, lp, d)
    y_sample = y[tp:].reshape(bs, ls, d)
    return (y_prompt, y_sample, jnp.stack(st_na_k, axis=1), jnp.stack(st_na_v, axis=1),
            jnp.stack(st_ga_k, axis=1), jnp.stack(st_ga_v, axis=1))
```

```python
import functools

import numpy as np
import jax
import jax.numpy as jnp
from jax import lax
from jax.experimental import pallas as pl
from jax.experimental.pallas import tpu as pltpu

F32 = jnp.float32
BF16 = jnp.bfloat16

LANES = 128
HEAD_DIM = 128
GRID_W = 64
NA_WIN_R = 8
NA_WIN_C = 16
N_KV_HEADS = 2
ROPE_THETA = 10000.0
CHUNK = 128
TOP_K = 4
SWIGLU_ALPHA = 1.702
SWIGLU_LIMIT = 7.0
MOE_BLOCK = 256
MOE_SUPER = 2 * MOE_BLOCK
NORM_EPS = 1e-6
NEG_INF = -1e30
ATT_SCALE = HEAD_DIM ** -0.5
NA_QROWS = 4
NA_KROWS = 12
VMEM_LIMIT = 56 * 1024 * 1024
SUBLANES = 8
ITEM_FILL, ITEM_COMPUTE, ITEM_NEW_WEIGHTS = 0, 1, 2
ISSUE_UNROLL = 8


def _cparams(sem):
    return pltpu.CompilerParams(dimension_semantics=sem, vmem_limit_bytes=VMEM_LIMIT)


def _pick(n, prefs):
    for p in prefs:
        if n % p == 0:
            return p
    return n


def _tile_mod_rows(tm, n_prompt, n_total, seq_sample):
    starts = np.arange(0, n_total, tm)
    return jnp.asarray(np.where(starts < n_prompt, 0, 1 + (starts - n_prompt) // seq_sample), jnp.int32)


def _rms(x):
    return x * lax.rsqrt(jnp.mean(x * x, axis=-1, keepdims=True) + NORM_EPS)


def _store_token_major(ref, val, first_row=0):
    rows, width = val.shape
    sub = width // LANES
    for s in range(sub):
        ref[pl.ds(first_row * sub + s, rows, stride=sub), :] = val[:, s * LANES:(s + 1) * LANES]


def _row_pitch(sub):
    return sub + SUBLANES if sub % (2 * SUBLANES) == 0 else sub


def _load_lane_chunk(buf, first_row, rows, pitch, s):
    return buf[pl.ds(first_row * pitch + s, rows, stride=pitch), :]


def _ada_kernel(cv_ref, w_ref, b_ref, o_ref):
    cv = cv_ref[...]
    a = (cv * jax.nn.sigmoid(cv)).astype(BF16)
    o_ref[0] = jnp.dot(a, w_ref[0].astype(BF16), preferred_element_type=F32) + b_ref[0]


def _ada_table(cv, w_ada, b_ada):
    depth, d, n = w_ada.shape
    tn = _pick(n, (512, 256, 128))
    return pl.pallas_call(
        _ada_kernel,
        grid=(depth, n // tn),
        in_specs=[pl.BlockSpec((8, d), lambda l, j: (0, 0)),
                  pl.BlockSpec((1, d, tn), lambda l, j: (l, 0, j)),
                  pl.BlockSpec((1, 1, tn), lambda l, j: (l, 0, j))],
        out_specs=pl.BlockSpec((1, 8, tn), lambda l, j: (l, 0, j)),
        out_shape=jax.ShapeDtypeStruct((depth, 8, n), F32),
        compiler_params=_cparams(("parallel", "parallel")),
        name="ada_table",
    )(cv, w_ada, b_ada.reshape(depth, 1, n))


def _normmod_kernel(rows_ref, x_ref, scale_ref, shift_ref, o_ref):
    del rows_ref
    o_ref[...] = (_rms(x_ref[...]) * (1.0 + scale_ref[0]) + shift_ref[0]).astype(o_ref.dtype)


def _normmod(x, mod, rows, tm, which_scale, which_shift, out_dtype):
    t, d = x.shape
    return pl.pallas_call(
        _normmod_kernel,
        grid_spec=pltpu.PrefetchScalarGridSpec(
            num_scalar_prefetch=1, grid=(t // tm,),
            in_specs=[pl.BlockSpec((tm, d), lambda i, r: (i, 0)),
                      pl.BlockSpec((1, 1, d), lambda i, r: (r[i] * 6 + which_scale, 0, 0)),
                      pl.BlockSpec((1, 1, d), lambda i, r: (r[i] * 6 + which_shift, 0, 0))],
            out_specs=pl.BlockSpec((tm, d), lambda i, r: (i, 0))),
        out_shape=jax.ShapeDtypeStruct((t, d), out_dtype),
        compiler_params=_cparams(("parallel",)),
        name="normmod",
    )(rows, x, mod, mod)


def _mm_kernel(a_ref, w_ref, o_ref):
    o_ref[...] = jnp.dot(a_ref[...], w_ref[...], preferred_element_type=F32).astype(o_ref.dtype)


def _matmul(a, w, out_dtype, tm, tn):
    m, k = a.shape
    n = w.shape[1]
    return pl.pallas_call(
        _mm_kernel,
        grid=(m // tm, n // tn),
        in_specs=[pl.BlockSpec((tm, k), lambda i, j: (i, 0)),
                  pl.BlockSpec((k, tn), lambda i, j: (0, j))],
        out_specs=pl.BlockSpec((tm, tn), lambda i, j: (i, j)),
        out_shape=jax.ShapeDtypeStruct((m, n), out_dtype),
        compiler_params=_cparams(("parallel", "parallel")),
        name="in_proj",
    )(a, w)


def _outproj_kernel(rows_ref, a_ref, w_ref, x_ref, g_ref, o_ref):
    del rows_ref
    o_ref[...] = x_ref[...] + g_ref[0] * jnp.dot(a_ref[...], w_ref[...], preferred_element_type=F32)


def _outproj(a, w, x, mod, rows, tm, tn):
    m, k = a.shape
    n = w.shape[1]
    return pl.pallas_call(
        _outproj_kernel,
        grid_spec=pltpu.PrefetchScalarGridSpec(
            num_scalar_prefetch=1, grid=(m // tm, n // tn),
            in_specs=[pl.BlockSpec((tm, k), lambda i, j, r: (i, 0)),
                      pl.BlockSpec((k, tn), lambda i, j, r: (0, j)),
                      pl.BlockSpec((tm, tn), lambda i, j, r: (i, j)),
                      pl.BlockSpec((1, 1, tn), lambda i, j, r: (r[i] * 6 + 2, 0, j))],
            out_specs=pl.BlockSpec((tm, tn), lambda i, j, r: (i, j))),
        out_shape=jax.ShapeDtypeStruct((m, n), F32),
        compiler_params=_cparams(("parallel", "parallel")),
        name="out_proj",
    )(rows, a, w, x, mod)


def _gatebranch_kernel(h_ref, wg_ref, bg_ref, y_ref, wb_ref, o_ref, acc_ref):
    br = pl.program_id(2)
    g = jnp.dot(h_ref[...], wg_ref[0], preferred_element_type=F32) + bg_ref[0]
    p = jnp.dot(y_ref[0], wb_ref[0], preferred_element_type=F32)
    v = jax.nn.sigmoid(g) * p

    @pl.when(br == 0)
    def _():
        acc_ref[...] = v

    @pl.when(br > 0)
    def _():
        acc_ref[...] += v

    @pl.when(br == pl.num_programs(2) - 1)
    def _():
        o_ref[...] = acc_ref[...].astype(o_ref.dtype)


def _gatebranch(h, w_gate, b_gate, ys, w_branch, tm, tn):
    t, d = h.shape
    nbr, bw, _ = w_branch.shape
    return pl.pallas_call(
        _gatebranch_kernel,
        grid=(t // tm, d // tn, nbr),
        in_specs=[pl.BlockSpec((tm, d), lambda i, j, b: (i, 0)),
                  pl.BlockSpec((1, d, tn), lambda i, j, b: (b, 0, j)),
                  pl.BlockSpec((1, 1, tn), lambda i, j, b: (b, 0, j)),
                  pl.BlockSpec((1, tm, bw), lambda i, j, b: (b, i, 0)),
                  pl.BlockSpec((1, bw, tn), lambda i, j, b: (b, 0, j))],
        out_specs=pl.BlockSpec((tm, tn), lambda i, j, b: (i, j)),
        out_shape=jax.ShapeDtypeStruct((t, d), BF16),
        scratch_shapes=[pltpu.VMEM((tm, tn), F32)],
        compiler_params=_cparams(("parallel", "parallel", "arbitrary")),
        name="gate_branch",
    )(h, w_gate, b_gate.reshape(nbr, 1, d), ys, w_branch)


def _softmax_parts(scores):
    m = scores[0].max(axis=-1, keepdims=True)
    for s in scores[1:]:
        m = jnp.maximum(m, s.max(axis=-1, keepdims=True))
    ps = [jnp.exp(s - m) for s in scores]
    l = ps[0].sum(axis=-1, keepdims=True)
    for p in ps[1:]:
        l = l + p.sum(axis=-1, keepdims=True)
    return ps, l


def _qk(q, k):
    return lax.dot_general(q, k, (((1,), (1,)), ((), ())), preferred_element_type=F32) * ATT_SCALE


def _store_head_state(ref, head, n_heads, val):
    ref[pl.ds(head, val.shape[0], stride=n_heads), :] = val


def _ctx_na_kernel(q_ref, k_ref, v_ref, o_ref, ks_ref, vs_ref, *, heads):
    k, v = k_ref[...], v_ref[...]
    _store_head_state(ks_ref, pl.program_id(1), heads, k)
    _store_head_state(vs_ref, pl.program_id(1), heads, v)
    s = _qk(q_ref[...].astype(BF16), k.astype(BF16))
    (p,), l = _softmax_parts([s])
    o_ref[...] = (jnp.dot(p.astype(BF16), v.astype(BF16), preferred_element_type=F32) / l).astype(o_ref.dtype)


def _ctx_na_attention(proj, nb, seq, heads, cq, ck, cv):
    state = jax.ShapeDtypeStruct((nb * seq * heads, HEAD_DIM), F32)
    state_spec = pl.BlockSpec((seq * heads, HEAD_DIM), lambda b, h: (b, 0))
    return pl.pallas_call(
        functools.partial(_ctx_na_kernel, heads=heads),
        grid=(nb, heads),
        in_specs=[pl.BlockSpec((seq, HEAD_DIM), lambda b, h: (b, cq + h)),
                  pl.BlockSpec((seq, HEAD_DIM), lambda b, h: (b, ck + h)),
                  pl.BlockSpec((seq, HEAD_DIM), lambda b, h: (b, cv + h))],
        out_specs=[pl.BlockSpec((seq, HEAD_DIM), lambda b, h: (b, h)), state_spec, state_spec],
        out_shape=[jax.ShapeDtypeStruct((nb * seq, heads * HEAD_DIM), BF16), state, state],
        compiler_params=_cparams(("parallel", "arbitrary")),
        name="ctx_na_attn",
    )(proj, proj, proj)


def _ctx_gqa_kernel(q_ref, k_ref, v_ref, qn_ref, kn_ref, o_ref, ks_ref, vs_ref, *, group):
    kv = pl.program_id(1) // group
    v = v_ref[...]
    kn = _rms(k_ref[...]) * kn_ref[...]
    _store_head_state(ks_ref, kv, N_KV_HEADS, kn)
    _store_head_state(vs_ref, kv, N_KV_HEADS, v)
    qn = _rms(q_ref[...]) * qn_ref[...]
    s = _qk(qn.astype(BF16), kn.astype(BF16))
    (p,), l = _softmax_parts([s])
    o_ref[...] = (jnp.dot(p.astype(BF16), v.astype(BF16), preferred_element_type=F32) / l).astype(o_ref.dtype)


def _ctx_gqa_attention(proj, q_norm, k_norm, nb, seq, heads, group, cq, ck, cv):
    state = jax.ShapeDtypeStruct((nb * seq * N_KV_HEADS, HEAD_DIM), F32)
    state_spec = pl.BlockSpec((seq * N_KV_HEADS, HEAD_DIM), lambda b, h: (b, 0))
    return pl.pallas_call(
        functools.partial(_ctx_gqa_kernel, group=group),
        grid=(nb, heads),
        in_specs=[pl.BlockSpec((seq, HEAD_DIM), lambda b, h: (b, cq + h)),
                  pl.BlockSpec((seq, HEAD_DIM), lambda b, h: (b, ck + h // group)),
                  pl.BlockSpec((seq, HEAD_DIM), lambda b, h: (b, cv + h // group)),
                  pl.BlockSpec((1, HEAD_DIM), lambda b, h: (0, 0)),
                  pl.BlockSpec((1, HEAD_DIM), lambda b, h: (0, 0))],
        out_specs=[pl.BlockSpec((seq, HEAD_DIM), lambda b, h: (b, h)), state_spec, state_spec],
        out_shape=[jax.ShapeDtypeStruct((nb * seq, heads * HEAD_DIM), BF16), state, state],
        compiler_params=_cparams(("parallel", "arbitrary")),
        name="ctx_gqa_attn",
    )(proj, proj, proj, q_norm.reshape(1, HEAD_DIM), k_norm.reshape(1, HEAD_DIM))


def _rope(x, cos_t, sin_t):
    return x * cos_t + pltpu.roll(x, HEAD_DIM // 2, 1) * sin_t


def _lat_gqa_kernel(q_ref, k_ref, v_ref, kc_ref, vc_ref, cos_ref, sin_ref, qn_ref, kn_ref, o_ref, kr_ref, *, tq):
    qi = pl.program_id(2)

    @pl.when(qi == 0)
    def _():
        kn = _rms(k_ref[...]) * kn_ref[...]
        kr_ref[...] = _rope(kn, cos_ref[...], sin_ref[...]).astype(BF16)

    rows = pl.ds(pl.multiple_of(qi * tq, tq), tq)
    qn = _rms(q_ref[...]) * qn_ref[...]
    qr = _rope(qn, cos_ref[rows, :], sin_ref[rows, :])
    s_ctx = _qk(qn.astype(BF16), kc_ref[0, 0].astype(BF16))
    s_lat = _qk(qr.astype(BF16), kr_ref[...])
    (p_ctx, p_lat), l = _softmax_parts([s_ctx, s_lat])
    out = (jnp.dot(p_ctx.astype(BF16), vc_ref[0, 0].astype(BF16), preferred_element_type=F32)
           + jnp.dot(p_lat.astype(BF16), v_ref[...].astype(BF16), preferred_element_type=F32))
    o_ref[...] = (out / l).astype(o_ref.dtype)


def _lat_gqa_attention(proj, cache_k, cache_v, layer, cos_t, sin_t, q_norm, k_norm,
                       row0, nb, seq, heads, group, cq, ck, cv):
    tq = _pick(seq, (512, 256, 128))
    nq = seq // tq
    rb = row0 // tq
    sb = row0 // seq
    past = cache_k.shape[2]
    return pl.pallas_call(
        functools.partial(_lat_gqa_kernel, tq=tq),
        grid=(nb, heads, nq),
        in_specs=[pl.BlockSpec((tq, HEAD_DIM), lambda b, h, i: (rb + b * nq + i, cq + h)),
                  pl.BlockSpec((seq, HEAD_DIM), lambda b, h, i: (sb + b, ck + h // group)),
                  pl.BlockSpec((seq, HEAD_DIM), lambda b, h, i: (sb + b, cv + h // group)),
                  pl.BlockSpec((1, 1, past, HEAD_DIM), lambda b, h, i: (b, layer, 0, h // group)),
                  pl.BlockSpec((1, 1, past, HEAD_DIM), lambda b, h, i: (b, layer, 0, h // group)),
                  pl.BlockSpec((seq, HEAD_DIM), lambda b, h, i: (0, 0)),
                  pl.BlockSpec((seq, HEAD_DIM), lambda b, h, i: (0, 0)),
                  pl.BlockSpec((1, HEAD_DIM), lambda b, h, i: (0, 0)),
                  pl.BlockSpec((1, HEAD_DIM), lambda b, h, i: (0, 0))],
        out_specs=pl.BlockSpec((tq, HEAD_DIM), lambda b, h, i: (b * nq + i, h)),
        out_shape=jax.ShapeDtypeStruct((nb * seq, heads * HEAD_DIM), BF16),
        scratch_shapes=[pltpu.VMEM((seq, HEAD_DIM), BF16)],
        compiler_params=_cparams(("parallel", "arbitrary", "arbitrary")),
        name="lat_gqa_attn",
    )(proj, proj, proj, cache_k, cache_v, cos_t, sin_t, q_norm.reshape(1, HEAD_DIM), k_norm.reshape(1, HEAD_DIM))


def _na_key_row0(i, n_rows):
    return jnp.clip(NA_QROWS * i - NA_WIN_R // 2, 0, n_rows - NA_KROWS)


def _lat_na_kernel(q_ref, k_ref, v_ref, kc_ref, vc_ref, bias_ref, o_ref, *, n_rows):
    i = pl.program_id(2)
    start = pl.multiple_of(_na_key_row0(i, n_rows) * GRID_W, GRID_W)
    keys = pl.ds(start, NA_KROWS * GRID_W)
    q = q_ref[...].astype(BF16)
    s_ctx = _qk(q, kc_ref[0, 0].astype(BF16))
    s_win = _qk(q, k_ref[keys, :].astype(BF16)) + bias_ref[0, 0, 0]
    (p_ctx, p_win), l = _softmax_parts([s_ctx, s_win])
    out = (jnp.dot(p_ctx.astype(BF16), vc_ref[0, 0].astype(BF16), preferred_element_type=F32)
           + jnp.dot(p_win.astype(BF16), v_ref[keys, :].astype(BF16), preferred_element_type=F32))
    o_ref[...] = (out / l).astype(o_ref.dtype)


def _na_bias_tables(rpb, n_rows):
    n_layers, heads = rpb.shape[:2]
    n_blocks = n_rows // NA_QROWS
    pad = GRID_W - NA_WIN_C
    padded = jnp.pad(rpb.astype(F32), ((0, 0), (0, 0), (0, 0), (pad, pad)))
    ccol = jnp.stack([padded[..., GRID_W - 1 - qc:2 * GRID_W - 1 - qc] for qc in range(GRID_W)], axis=-2)
    qc = np.arange(GRID_W)[:, None]
    kc = np.arange(GRID_W)[None, :]
    cs = np.clip(qc - NA_WIN_C // 2, 0, GRID_W - NA_WIN_C)
    ccol = jnp.where((kc >= cs) & (kc < cs + NA_WIN_C), ccol, NEG_INF)
    masked = jnp.full((n_layers, heads, GRID_W, GRID_W), NEG_INF, F32)
    kinds = []
    for i in (0, min(2, n_blocks - 1), n_blocks - 1):
        k0 = int(np.clip(NA_QROWS * i - NA_WIN_R // 2, 0, n_rows - NA_KROWS))
        q_rows = []
        for qr in range(NA_QROWS * i, NA_QROWS * (i + 1)):
            rs = int(np.clip(qr - NA_WIN_R // 2, 0, n_rows - NA_WIN_R))
            q_rows.append(jnp.concatenate(
                [ccol[:, :, kr - qr + NA_WIN_R - 1] if rs <= kr < rs + NA_WIN_R else masked
                 for kr in range(k0, k0 + NA_KROWS)], axis=-1))
        kinds.append(jnp.concatenate(q_rows, axis=-2))
    return jnp.stack(kinds, axis=2)


def _lat_na_attention(proj, cache_k, cache_v, layer, bias, row0, nb, seq, heads, cq, ck, cv):
    n_rows = seq // GRID_W
    tq = NA_QROWS * GRID_W
    nq = seq // tq
    rb = row0 // tq
    sb = row0 // seq
    past = cache_k.shape[2]
    kw = NA_KROWS * GRID_W

    def bias_map(b, h, i):
        return (layer, h, jnp.where(i == 0, 0, jnp.where(i == nq - 1, 2, 1)), 0, 0)

    return pl.pallas_call(
        functools.partial(_lat_na_kernel, n_rows=n_rows),
        grid=(nb, heads, nq),
        in_specs=[pl.BlockSpec((tq, HEAD_DIM), lambda b, h, i: (rb + b * nq + i, cq + h)),
                  pl.BlockSpec((seq, HEAD_DIM), lambda b, h, i: (sb + b, ck + h)),
                  pl.BlockSpec((seq, HEAD_DIM), lambda b, h, i: (sb + b, cv + h)),
                  pl.BlockSpec((1, 1, past, HEAD_DIM), lambda b, h, i: (b, layer, 0, h)),
                  pl.BlockSpec((1, 1, past, HEAD_DIM), lambda b, h, i: (b, layer, 0, h)),
                  pl.BlockSpec((1, 1, 1, tq, kw), bias_map)],
        out_specs=pl.BlockSpec((tq, HEAD_DIM), lambda b, h, i: (b * nq + i, h)),
        out_shape=jax.ShapeDtypeStruct((nb * seq, heads * HEAD_DIM), BF16),
        compiler_params=_cparams(("parallel", "parallel", "arbitrary")),
        name="lat_na_attn",
    )(proj, proj, proj, cache_k, cache_v, bias)


def _conv_kernel(b_ref, c_ref, x_ref, w_ref, o_ref):
    p = c_ref[...] * x_ref[...]
    n = p.shape[0]
    row = lax.broadcasted_iota(jnp.int32, p.shape, 0)
    prev = jnp.where(row == 0, 0.0, pltpu.roll(p, 1, 0))
    nxt = jnp.where(row == n - 1, 0.0, pltpu.roll(p, n - 1, 0))
    w = w_ref[...]
    o_ref[...] = (b_ref[...] * (prev * w[0:1] + p * w[1:2] + nxt * w[2:3])).astype(o_ref.dtype)


def _short_conv(proj, conv_w, row0, nb, seq, width, off_b, off_c, off_x):
    tc = _pick(width, (256, 128))
    assert off_b % tc == 0 and off_c % tc == 0 and off_x % tc == 0
    sb = row0 // seq
    k = width // tc
    cb, cc, cx = off_b // tc, off_c // tc, off_x // tc
    return pl.pallas_call(
        _conv_kernel,
        grid=(nb, k),
        in_specs=[pl.BlockSpec((seq, tc), lambda b, j: (sb + b, cb + j)),
                  pl.BlockSpec((seq, tc), lambda b, j: (sb + b, cc + j)),
                  pl.BlockSpec((seq, tc), lambda b, j: (sb + b, cx + j)),
                  pl.BlockSpec((3, tc), lambda b, j: (0, j))],
        out_specs=pl.BlockSpec((seq, tc), lambda b, j: (b, j)),
        out_shape=jax.ShapeDtypeStruct((nb * seq, width), BF16),
        compiler_params=_cparams(("parallel", "parallel")),
        name="short_conv",
    )(proj, proj, proj, conv_w)


def _sgu_kernel(*refs, pieces):
    u_refs, v_refs = refs[:pieces], refs[pieces:2 * pieces]
    g_ref, b_ref, w_ref, bt_ref, o_ref = refs[2 * pieces:]
    rows, width = o_ref.shape
    groups = w_ref.shape[0]
    gw = width // groups
    pw = width // pieces
    for c in range(rows // CHUNK):
        rs = slice(c * CHUNK, (c + 1) * CHUNK)
        v = jnp.concatenate([r[rs, :] for r in v_refs], axis=-1)
        mu = jnp.mean(v, axis=-1, keepdims=True)
        var = jnp.mean(jnp.square(v - mu), axis=-1, keepdims=True)
        vn = ((v - mu) * lax.rsqrt(var + NORM_EPS)) * g_ref[...] + b_ref[...]
        for g in range(groups):
            cs = slice(g * gw, (g + 1) * gw)
            mixed = jnp.dot(w_ref[g].astype(BF16), vn[:, cs].astype(BF16), preferred_element_type=F32)
            mixed = mixed + bt_ref[:, g:g + 1]
            piece, lo = divmod(g * gw, pw)
            o_ref[rs, cs] = (u_refs[piece][rs, lo:lo + gw] * mixed).astype(o_ref.dtype)


def _spatial_gating(proj, ln_g, ln_b, w_s, b_s, width, off_u, off_v):
    t = proj.shape[0]
    rows = _pick(t, (256, 128))
    groups = w_s.shape[0]
    pw = int(np.gcd.reduce([width, off_u, off_v]))
    pieces = width // pw
    assert pw % (width // groups) == 0
    col_specs = [pl.BlockSpec((rows, pw), functools.partial(lambda i, c: (i, c), c=off // pw + p))
                 for off in (off_u, off_v) for p in range(pieces)]
    return pl.pallas_call(
        functools.partial(_sgu_kernel, pieces=pieces),
        grid=(t // rows,),
        in_specs=col_specs + [pl.BlockSpec((1, width), lambda i: (0, 0)),
                              pl.BlockSpec((1, width), lambda i: (0, 0)),
                              pl.BlockSpec((groups, CHUNK, CHUNK), lambda i: (0, 0, 0)),
                              pl.BlockSpec((CHUNK, groups), lambda i: (0, 0))],
        out_specs=pl.BlockSpec((rows, width), lambda i: (i, 0)),
        out_shape=jax.ShapeDtypeStruct((t, width), BF16),
        compiler_params=_cparams(("parallel",)),
        name="spatial_gating",
    )(*([proj] * (2 * pieces)), ln_g.reshape(1, width), ln_b.reshape(1, width), w_s, b_s.T)


def _router_kernel(rows_ref, x_ref, scale_ref, shift_ref, wr_ref, br_ref,
                   h_ref, idx_ref, gate_ref, rank_ref, cnt_ref, carry_ref):
    del rows_ref
    i = pl.program_id(0)

    @pl.when(i == 0)
    def _():
        carry_ref[...] = jnp.zeros_like(carry_ref)

    h = _rms(x_ref[...]) * (1.0 + scale_ref[0]) + shift_ref[0]
    _store_token_major(h_ref, h)
    logits = jnp.dot(h, wr_ref[...], preferred_element_type=F32, precision=lax.Precision.HIGHEST) + br_ref[...]
    tm, n_exp = logits.shape
    lane_e = lax.broadcasted_iota(jnp.int32, (tm, n_exp), 1)
    lane_o = lax.broadcasted_iota(jnp.int32, (tm, LANES), 1)

    work = logits
    vals, sels = [], []
    for _ in range(TOP_K):
        m = work.max(axis=-1, keepdims=True)
        sel = jnp.min(jnp.where(work == m, lane_e, n_exp), axis=-1, keepdims=True)
        vals.append(m)
        sels.append(sel)
        work = jnp.where(lane_e == sel, -jnp.inf, work)
    exps = [jnp.exp(v - vals[0]) for v in vals]
    denom = exps[0]
    for e in exps[1:]:
        denom = denom + e

    onehot = jnp.zeros((tm, n_exp), F32)
    for sel in sels:
        onehot = onehot + (lane_e == sel).astype(F32)
    r = lax.broadcasted_iota(jnp.int32, (tm, tm), 0)
    c = lax.broadcasted_iota(jnp.int32, (tm, tm), 1)
    before = (r > c).astype(BF16)
    prefix = jnp.dot(before, onehot.astype(BF16), preferred_element_type=F32) + carry_ref[...]
    carry = carry_ref[...] + onehot.sum(axis=0, keepdims=True)
    carry_ref[...] = carry
    cnt_ref[...] = carry

    idx_out = jnp.zeros((tm, LANES), jnp.int32)
    gate_out = jnp.zeros((tm, LANES), F32)
    rank_out = jnp.zeros((tm, LANES), jnp.int32)
    for k in range(TOP_K):
        rank = jnp.sum(jnp.where(lane_e == sels[k], prefix, 0.0), axis=-1, keepdims=True).astype(jnp.int32)
        idx_out = jnp.where(lane_o == k, sels[k], idx_out)
        gate_out = jnp.where(lane_o == k, exps[k] / denom, gate_out)
        rank_out = jnp.where(lane_o == k, rank, rank_out)
    idx_ref[...] = idx_out
    gate_ref[...] = gate_out
    rank_ref[...] = rank_out


def _router(x, mod, rows, tm, w_router, b_router):
    t, d = x.shape
    n_exp = w_router.shape[1]
    return pl.pallas_call(
        _router_kernel,
        grid_spec=pltpu.PrefetchScalarGridSpec(
            num_scalar_prefetch=1, grid=(t // tm,),
            in_specs=[pl.BlockSpec((tm, d), lambda i, r: (i, 0)),
                      pl.BlockSpec((1, 1, d), lambda i, r: (r[i] * 6 + 4, 0, 0)),
                      pl.BlockSpec((1, 1, d), lambda i, r: (r[i] * 6 + 3, 0, 0)),
                      pl.BlockSpec((d, n_exp), lambda i, r: (0, 0)),
                      pl.BlockSpec((1, n_exp), lambda i, r: (0, 0))],
            out_specs=[pl.BlockSpec((tm * d // LANES, LANES), lambda i, r: (i, 0)),
                       pl.BlockSpec((tm, LANES), lambda i, r: (i, 0)),
                       pl.BlockSpec((tm, LANES), lambda i, r: (i, 0)),
                       pl.BlockSpec((tm, LANES), lambda i, r: (i, 0)),
                       pl.BlockSpec((1, n_exp), lambda i, r: (0, 0))],
            scratch_shapes=[pltpu.VMEM((1, n_exp), F32)]),
        out_shape=[jax.ShapeDtypeStruct((t * d // LANES, LANES), F32),
                   jax.ShapeDtypeStruct((t, LANES), jnp.int32),
                   jax.ShapeDtypeStruct((t, LANES), F32),
                   jax.ShapeDtypeStruct((t, LANES), jnp.int32),
                   jax.ShapeDtypeStruct((1, n_exp), F32)],
        compiler_params=_cparams(("arbitrary",)),
        name="router",
    )(rows, x, mod, mod, w_router, b_router.reshape(1, n_exp))


def _wait_rows(buf_ref, n_rows, sub, sem):
    done = buf_ref.at[pl.ds(0, n_rows * sub), :]
    pltpu.make_async_copy(done, done, sem).wait()


def _gather_kernel(tok_ref, used_ref, src_ref, o_ref, buf0, buf1, sems):
    i = pl.program_id(0)
    last = pl.num_programs(0) - 1
    n_rows = o_ref.shape[0]
    sub = src_ref.shape[1]
    pitch = _row_pitch(sub)
    bufs = (buf0, buf1)

    def issue(block, slot):
        def body(g, carry):
            for j in range(ISSUE_UNROLL):
                r = g * ISSUE_UNROLL + j
                rows = pl.ds(pl.multiple_of(r * pitch, SUBLANES), sub)
                copy = pltpu.make_async_copy(src_ref.at[tok_ref[block * n_rows + r]], bufs[slot].at[rows, :], sems.at[slot])
                copy.start(priority=j % 2)
            return carry
        lax.fori_loop(0, n_rows // ISSUE_UNROLL, body, 0)

    for slot in (0, 1):
        @pl.when(i % 2 == slot)
        def _(slot=slot):
            if slot == 0:
                @pl.when(jnp.logical_and(i == 0, used_ref[0] == 1))
                def _():
                    issue(0, 0)

            @pl.when(jnp.logical_and(i < last, used_ref[jnp.minimum(i + 1, last)] == 1))
            def _():
                issue(i + 1, 1 - slot)

            @pl.when(used_ref[i] == 1)
            def _():
                _wait_rows(bufs[slot], n_rows, sub, sems.at[slot])
                for s in range(sub):
                    chunk = _load_lane_chunk(bufs[slot], 0, n_rows, pitch, s)
                    o_ref[:, s * LANES:(s + 1) * LANES] = chunk.astype(o_ref.dtype)

    @pl.when(used_ref[i] == 0)
    def _():
        o_ref[...] = jnp.zeros_like(o_ref)


def _gather_rows(slot_tok, block_used, src3, n_slots):
    _, sub, _ = src3.shape
    d = sub * LANES
    buf = pltpu.VMEM((MOE_BLOCK * _row_pitch(sub), LANES), F32)
    return pl.pallas_call(
        _gather_kernel,
        grid_spec=pltpu.PrefetchScalarGridSpec(
            num_scalar_prefetch=2, grid=(n_slots // MOE_BLOCK,),
            in_specs=[pl.BlockSpec(memory_space=pl.ANY)],
            out_specs=pl.BlockSpec((MOE_BLOCK, d), lambda i, t, u: (i, 0)),
            scratch_shapes=[buf, buf, pltpu.SemaphoreType.DMA((2,))]),
        out_shape=jax.ShapeDtypeStruct((n_slots, d), BF16),
        compiler_params=_cparams(("arbitrary",)),
        name="expert_gather",
    )(slot_tok, block_used, src3)


N_SCHED = 8


def _stream_weights(sched, tile_copies, stage, wbf_refs):
    e_ref, t_ref, flag_ref, ne_ref, nt_ref = sched[0], sched[1], sched[4], sched[6], sched[7]
    w = pl.program_id(0)

    @pl.when(flag_ref[w] == ITEM_NEW_WEIGHTS)
    def _():
        @pl.when(w == 0)
        def _():
            for c in tile_copies(e_ref[0], t_ref[0]):
                c.start()

        for c in tile_copies(e_ref[w], t_ref[w]):
            c.wait()
        for p, ref in enumerate(wbf_refs):
            ref[...] = stage[p].astype(BF16)

        @pl.when(ne_ref[w] >= 0)
        def _():
            for c in tile_copies(ne_ref[w], nt_ref[w]):
                c.start()


def _run_item(sched, in_ref, compute, write):
    w = pl.program_id(0)
    real = sched[4][w] != ITEM_FILL
    first_only = sched[5][w] == 1
    half = in_ref.shape[0] // 2

    @pl.when(jnp.logical_and(real, jnp.logical_not(first_only)))
    def _():
        y = compute(in_ref[...])
        write(0, y[:half])
        write(1, y[half:])

    @pl.when(jnp.logical_and(real, first_only))
    def _():
        write(0, compute(in_ref[:half, :]))
        write(1, None)

    @pl.when(jnp.logical_not(real))
    def _():
        write(0, None)
        write(1, None)


def _expert_up_kernel(*refs, layer, tf, nf):
    sched = refs[:N_SCHED]
    x_ref, w_hbm, bg_ref, bu_ref, o_ref, stage, wg_bf, wu_bf, sems = refs[N_SCHED:]
    half = o_ref.shape[0] // 2

    def tile_copies(e, t):
        cols_g = pl.ds(pl.multiple_of(t * tf, tf), tf)
        cols_u = pl.ds(pl.multiple_of((nf + t) * tf, tf), tf)
        return (pltpu.make_async_copy(w_hbm.at[layer, e, :, cols_g], stage.at[0], sems.at[0]),
                pltpu.make_async_copy(w_hbm.at[layer, e, :, cols_u], stage.at[1], sems.at[1]))

    _stream_weights(sched, tile_copies, stage, (wg_bf, wu_bf))

    def compute(x):
        g = jnp.dot(x, wg_bf[...], preferred_element_type=F32) + bg_ref[0, 0]
        u = jnp.dot(x, wu_bf[...], preferred_element_type=F32) + bu_ref[0, 0]
        g = jnp.minimum(g, SWIGLU_LIMIT)
        u = jnp.clip(u, -SWIGLU_LIMIT, SWIGLU_LIMIT)
        return (g * jax.nn.sigmoid(SWIGLU_ALPHA * g) * (u + 1.0)).astype(BF16)

    def write(which, val):
        rows = pl.ds(which * half, half)
        o_ref[rows, :] = jnp.zeros((half, o_ref.shape[1]), o_ref.dtype) if val is None else val

    _run_item(sched, x_ref, compute, write)


def _expert_up(sched, xs, w_gu, b_gu4, layer, tf):
    n_slots, d = xs.shape
    ff = w_gu.shape[3] // 2
    nf = ff // tf
    n_items = sched[0].shape[0]
    return pl.pallas_call(
        functools.partial(_expert_up_kernel, layer=layer, tf=tf, nf=nf),
        grid_spec=pltpu.PrefetchScalarGridSpec(
            num_scalar_prefetch=N_SCHED, grid=(n_items,),
            in_specs=[pl.BlockSpec((MOE_SUPER, d), lambda w, e, t, b, *_: (b[w], 0)),
                      pl.BlockSpec(memory_space=pl.ANY),
                      pl.BlockSpec((1, 1, 1, tf), lambda w, e, t, *_: (layer, e[w], 0, t[w])),
                      pl.BlockSpec((1, 1, 1, tf), lambda w, e, t, *_: (layer, e[w], 0, nf + t[w]))],
            out_specs=pl.BlockSpec((MOE_SUPER, tf), lambda w, e, t, b, o, *_: (b[w], o[w])),
            scratch_shapes=[pltpu.VMEM((2, d, tf), F32), pltpu.VMEM((d, tf), BF16), pltpu.VMEM((d, tf), BF16),
                            pltpu.SemaphoreType.DMA((2,))]),
        out_shape=jax.ShapeDtypeStruct((n_slots, ff), BF16),
        compiler_params=_cparams(("arbitrary",)),
        name="expert_up",
    )(*sched, xs, w_gu, b_gu4, b_gu4)


def _expert_down_kernel(*refs, layer, tn):
    sched = refs[:N_SCHED]
    h_ref, w_hbm, bd_ref, o_ref, stage, wd_bf, sems = refs[N_SCHED:]
    half = h_ref.shape[0] // 2
    half_rows = o_ref.shape[0] // 2

    def tile_copies(e, t):
        cols = pl.ds(pl.multiple_of(t * tn, tn), tn)
        return (pltpu.make_async_copy(w_hbm.at[layer, e, :, cols], stage.at[0], sems.at[0]),)

    _stream_weights(sched, tile_copies, stage, (wd_bf,))

    def compute(h):
        return jnp.dot(h, wd_bf[...], preferred_element_type=F32) + bd_ref[0, 0]

    def write(which, val):
        if val is None:
            o_ref[pl.ds(which * half_rows, half_rows), :] = jnp.zeros((half_rows, o_ref.shape[1]), o_ref.dtype)
        else:
            _store_token_major(o_ref, val, first_row=which * half)

    _run_item(sched, h_ref, compute, write)


def _expert_down(sched, hs, w_down, b_down4, layer, tn):
    n_slots, ff = hs.shape
    d = w_down.shape[3]
    n_items = sched[0].shape[0]
    n_blocks = n_slots // MOE_SUPER
    planes, sub = d // tn, tn // LANES
    out = pl.pallas_call(
        functools.partial(_expert_down_kernel, layer=layer, tn=tn),
        grid_spec=pltpu.PrefetchScalarGridSpec(
            num_scalar_prefetch=N_SCHED, grid=(n_items,),
            in_specs=[pl.BlockSpec((MOE_SUPER, ff), lambda w, e, t, b, *_: (b[w], 0)),
                      pl.BlockSpec(memory_space=pl.ANY),
                      pl.BlockSpec((1, 1, 1, tn), lambda w, e, t, *_: (layer, e[w], 0, t[w]))],
            out_specs=pl.BlockSpec((MOE_SUPER * sub, LANES), lambda w, e, t, b, o, *_: (o[w] * n_blocks + b[w], 0)),
            scratch_shapes=[pltpu.VMEM((1, ff, tn), F32), pltpu.VMEM((ff, tn), BF16),
                            pltpu.SemaphoreType.DMA((1,))]),
        out_shape=jax.ShapeDtypeStruct((planes * n_slots * sub, LANES), F32),
        compiler_params=_cparams(("arbitrary",)),
        name="expert_down",
    )(*sched, hs, w_down, b_down4)
    return out.reshape(planes, n_slots, sub, LANES)


def _expert_schedule(counts, n_blk_e, blk_start, n_tiles, n_blocks):
    i32 = jnp.int32
    n_exp = n_blk_e.shape[0]
    items = n_blk_e * n_tiles
    item_end = jnp.cumsum(items)
    n_used = jnp.sum(n_blk_e)
    total = n_used * n_tiles
    w = jnp.arange(n_tiles * n_blocks, dtype=i32)
    wc = jnp.clip(w, 0, jnp.maximum(total - 1, 0))
    e = jnp.minimum(jnp.sum(item_end[None, :] <= wc[:, None], axis=1), n_exp - 1).astype(i32)
    of_e = e[:, None] == jnp.arange(n_exp, dtype=i32)[None, :]

    def lookup(table):
        return jnp.sum(jnp.where(of_e, table[None, :], 0), axis=1)

    local = wc - lookup(item_end - items)
    nbe = jnp.maximum(lookup(n_blk_e), 1)
    tile = local // nbe
    blk_start = lookup(blk_start)
    valid = w < total
    spare = jnp.maximum(w - total, 0)
    n_unused = jnp.maximum(n_blocks - n_used, 1)
    blk = jnp.where(valid, blk_start + local % nbe, n_used + spare % n_unused)
    out_tile = jnp.where(valid, tile, spare // n_unused)
    flag = jnp.where(valid, jnp.where(local % nbe == 0, ITEM_NEW_WEIGHTS, ITEM_COMPUTE), ITEM_FILL)
    first_only = valid & (lookup(counts) - (local % nbe) * MOE_SUPER <= MOE_SUPER // 2)
    experts = jnp.arange(n_exp, dtype=i32)
    nonempty = n_blk_e > 0
    before = jnp.cumsum(nonempty) - nonempty
    later = nonempty[None, :] & (experts[None, :] > experts[:, None])
    next_nonempty = jnp.min(jnp.where(later, experts[None, :], n_exp), axis=1)
    group = lookup(before) * n_tiles + tile
    last_tile = tile + 1 >= n_tiles
    next_e = jnp.where(last_tile, lookup(next_nonempty), e)
    next_e = jnp.where(group + 1 < jnp.sum(nonempty) * n_tiles, next_e, -1)
    next_tile = jnp.where(last_tile, 0, tile + 1)
    return tuple(a.astype(i32) for a in (e, tile, blk, out_tile, flag, first_only, next_e, next_tile))


def _combine_kernel(dest_ref, rows_ref, y_ref, gate_ref, x_ref, g2_ref, o_ref, *scratch):
    del rows_ref
    planes = y_ref.shape[0]
    bufs, sems = scratch[:-1], scratch[-1]
    i = pl.program_id(0)
    tm = x_ref.shape[0]
    sub = y_ref.shape[2]
    pitch = _row_pitch(sub)

    def issue(tile, slot):
        def body(g, carry):
            for u in range(ISSUE_UNROLL):
                p = g * ISSUE_UNROLL + u
                k = p // tm
                t = p - k * tm
                row = dest_ref[(tile * tm + t) * TOP_K + k]
                rows = pl.ds(pl.multiple_of(p * pitch, SUBLANES), sub)
                for j in range(planes):
                    b = slot * planes + j
                    copy = pltpu.make_async_copy(y_ref.at[j, row], bufs[b].at[rows, :], sems.at[b])
                    copy.start(priority=(u * planes + j) % 2)
            return carry
        lax.fori_loop(0, TOP_K * tm // ISSUE_UNROLL, body, 0)

    for slot in (0, 1):
        @pl.when(i % 2 == slot)
        def _(slot=slot):
            if slot == 0:
                @pl.when(i == 0)
                def _():
                    issue(0, 0)

            @pl.when(i + 1 < pl.num_programs(0))
            def _():
                issue(i + 1, 1 - slot)

            gate = gate_ref[...]
            for j in range(planes):
                buf = bufs[slot * planes + j]
                _wait_rows(buf, TOP_K * tm, sub, sems.at[slot * planes + j])
                for s in range(sub):
                    c0 = (j * sub + s) * LANES
                    acc = gate[:, 0:1] * _load_lane_chunk(buf, 0, tm, pitch, s)
                    for k in range(1, TOP_K):
                        acc = acc + gate[:, k:k + 1] * _load_lane_chunk(buf, k * tm, tm, pitch, s)
                    o_ref[:, c0:c0 + LANES] = x_ref[:, c0:c0 + LANES] + g2_ref[0, :, c0:c0 + LANES] * acc


def _combine(dest_flat, rows, y4, gates, x, mod, tm):
    t, d = x.shape
    planes, _, sub, _ = y4.shape
    bufs = [pltpu.VMEM((TOP_K * tm * _row_pitch(sub), LANES), F32) for _ in range(2 * planes)]
    return pl.pallas_call(
        _combine_kernel,
        grid_spec=pltpu.PrefetchScalarGridSpec(
            num_scalar_prefetch=2, grid=(t // tm,),
            in_specs=[pl.BlockSpec(memory_space=pl.ANY),
                      pl.BlockSpec((tm, LANES), lambda i, dst, r: (i, 0)),
                      pl.BlockSpec((tm, d), lambda i, dst, r: (i, 0)),
                      pl.BlockSpec((1, 1, d), lambda i, dst, r: (r[i] * 6 + 5, 0, 0))],
            out_specs=pl.BlockSpec((tm, d), lambda i, dst, r: (i, 0)),
            scratch_shapes=bufs + [pltpu.SemaphoreType.DMA((2 * planes,))]),
        out_shape=jax.ShapeDtypeStruct((t, d), F32),
        compiler_params=_cparams(("arbitrary",)),
        name="expert_combine",
    )(dest_flat, rows, y4, gates, x, mod)


def _final_kernel(x_ref, w_ref, o_ref):
    o_ref[...] = _rms(x_ref[...]) * w_ref[...]


def _final_norm(x, w, tm, row0, n_rows):
    d = x.shape[1]
    first = row0 // tm
    return pl.pallas_call(
        _final_kernel,
        grid=(n_rows // tm,),
        in_specs=[pl.BlockSpec((tm, d), lambda i: (first + i, 0)), pl.BlockSpec((1, d), lambda i: (0, 0))],
        out_specs=pl.BlockSpec((tm, d), lambda i: (i, 0)),
        out_shape=jax.ShapeDtypeStruct((n_rows, d), F32),
        compiler_params=_cparams(("parallel",)),
        name="final_norm",
    )(x, w.reshape(1, d))


def _rope_tables(length):
    pairs = HEAD_DIM // 4
    t = np.arange(length)
    row = (t // GRID_W).astype(np.float32)
    col = (t % GRID_W).astype(np.float32)
    inv = jnp.asarray(ROPE_THETA, F32) ** (-jnp.arange(pairs, dtype=F32) / pairs)
    ang = jnp.concatenate([jnp.asarray(row)[:, None] * inv, jnp.asarray(col)[:, None] * inv], axis=-1)
    cos, sin = jnp.cos(ang), jnp.sin(ang)
    return jnp.concatenate([cos, cos], axis=-1), jnp.concatenate([-sin, sin], axis=-1)


def kernel(x_prompt, x_sample, c, cache_na_k, cache_na_v, cache_gqa_k, cache_gqa_v, c_ctx, w_ada, b_ada, w_in, q_norm, k_norm, na_rpb, conv_w, sgu_w, sgu_b, sgu_ln_g, sgu_ln_b, w_branch, w_gate, b_gate, w_out, w_router, b_router, w_gu, b_gu, w_down, b_down, final_norm):
    bp, lp, d = x_prompt.shape
    bs, ls, _ = x_sample.shape
    depth = w_ada.shape[0]
    past = cache_na_k.shape[2]
    tp, ts = bp * lp, bs * ls
    t = tp + ts
    bw = w_branch.shape[2]
    heads = bw // HEAD_DIM
    group = heads // N_KV_HEADS
    n_exp = w_router.shape[2]
    ff = w_down.shape[2]
    n_rows = ls // GRID_W
    assert bw % LANES == 0 and sgu_w.shape[1] * CHUNK == bw and sgu_w.shape[2] == CHUNK

    widths = (bw, bw, bw, heads * HEAD_DIM, N_KV_HEADS * HEAD_DIM, N_KV_HEADS * HEAD_DIM, bw, bw, bw, bw, bw)
    offs = np.concatenate([[0], np.cumsum(widths)])
    blk = [int(o) // LANES for o in offs]
    offs = [int(o) for o in offs]
    assert tp % ls == 0

    x = jnp.concatenate([x_prompt.reshape(tp, d), x_sample.reshape(ts, d)], axis=0)
    cv = jnp.concatenate([c_ctx[None, :], c, jnp.zeros((8 - 1 - bs, d), F32)], axis=0)
    mod_all = _ada_table(cv, w_ada, b_ada)
    cos_t, sin_t = _rope_tables(ls)
    bias = _na_bias_tables(na_rpb, n_rows)
    cna_k = cache_na_k.reshape(bs, depth, past, heads * HEAD_DIM)
    cna_v = cache_na_v.reshape(bs, depth, past, heads * HEAD_DIM)
    cga_k = cache_gqa_k.reshape(bs, depth, past, N_KV_HEADS * HEAD_DIM)
    cga_v = cache_gqa_v.reshape(bs, depth, past, N_KV_HEADS * HEAD_DIM)

    tm_n = _pick(lp, (256, 128))
    tm_m = _pick(ls, (1024, 512, 256))
    tm_m = tm_m if tp % tm_m == 0 else tm_n
    rows_n = _tile_mod_rows(tm_n, tp, t, ls)
    rows_m = _tile_mod_rows(tm_m, tp, t, ls)
    tm_c = CHUNK
    rows_c = _tile_mod_rows(tm_c, tp, t, ls)
    tn = _pick(d, (512, 256, 128))
    tn_in = _pick(w_in.shape[2], (512, 256, 128))
    tf = _pick(ff, (512, 256, 128))
    tn_d = _pick(d, (2048, 1024, 512, 256, 128))
    n_assign = t * TOP_K
    n_blocks = -(-n_assign // MOE_SUPER) + n_exp
    n_slots = n_blocks * MOE_SUPER
    b_gu4 = b_gu.reshape(depth, n_exp, 1, 2 * ff)
    b_down4 = b_down.reshape(depth, n_exp, 1, d)

    st_na_k, st_na_v, st_ga_k, st_ga_v = [], [], [], []
    for l in range(depth):
        mod = mod_all[l].reshape(8 * 6, 1, d)
        h = _normmod(x, mod, rows_n, tm_n, 1, 0, BF16)
        proj = _matmul(h, w_in[l].astype(BF16), F32, tm_m, tn_in)

        y_na_p, na_k_p, na_v_p = _ctx_na_attention(proj, bp, lp, heads, blk[0], blk[1], blk[2])
        y_ga_p, ga_k_p, ga_v_p = _ctx_gqa_attention(proj, q_norm[l], k_norm[l], bp, lp, heads, group,
                                                    blk[3], blk[4], blk[5])
        st_na_k.append(na_k_p.reshape(bp, lp, heads, HEAD_DIM))
        st_na_v.append(na_v_p.reshape(bp, lp, heads, HEAD_DIM))
        st_ga_k.append(ga_k_p.reshape(bp, lp, N_KV_HEADS, HEAD_DIM))
        st_ga_v.append(ga_v_p.reshape(bp, lp, N_KV_HEADS, HEAD_DIM))

        y_na_s = _lat_na_attention(proj, cna_k, cna_v, l, bias, tp, bs, ls, heads, blk[0], blk[1], blk[2])
        y_ga_s = _lat_gqa_attention(proj, cga_k, cga_v, l, cos_t, sin_t, q_norm[l], k_norm[l],
                                    tp, bs, ls, heads, group, blk[3], blk[4], blk[5])

        y_sc = jnp.concatenate([
            _short_conv(proj, conv_w[l], 0, bp, lp, bw, offs[6], offs[7], offs[8]),
            _short_conv(proj, conv_w[l], tp, bs, ls, bw, offs[6], offs[7], offs[8])], axis=0)
        y_sg = _spatial_gating(proj, sgu_ln_g[l], sgu_ln_b[l], sgu_w[l], sgu_b[l], bw, offs[9], offs[10])
        ys = jnp.stack([jnp.concatenate([y_na_p, y_na_s], axis=0), y_sc, y_sg,
                        jnp.concatenate([y_ga_p, y_ga_s], axis=0)], axis=0)

        merged = _gatebranch(h, w_gate[l].astype(BF16), b_gate[l], ys, w_branch[l].astype(BF16), tm_m, tn)
        x = _outproj(merged, w_out[l].astype(BF16), x, mod, rows_m, tm_m, tn)

        h2, top_idx, gates, rank, counts = _router(x, mod, rows_n, tm_n, w_router[l], b_router[l])
        counts = counts[0].astype(jnp.int32)
        n_blk_e = (counts + MOE_SUPER - 1) // MOE_SUPER
        blk_start = jnp.cumsum(n_blk_e) - n_blk_e
        experts = jnp.arange(n_exp, dtype=jnp.int32)
        of_e = top_idx[:, :TOP_K, None] == experts
        dest = (jnp.sum(jnp.where(of_e, blk_start, 0), axis=-1) * MOE_SUPER + rank[:, :TOP_K]).reshape(-1)
        slot_tok = jnp.zeros((n_slots,), jnp.int32).at[dest].set(jnp.arange(n_assign, dtype=jnp.int32) // TOP_K)
        gblk = jnp.arange(n_slots // MOE_BLOCK, dtype=jnp.int32) * MOE_BLOCK
        in_e = (gblk[:, None] >= blk_start * MOE_SUPER) & (gblk[:, None] < (blk_start + n_blk_e) * MOE_SUPER)
        block_used = jnp.any(in_e & (gblk[:, None] - blk_start * MOE_SUPER < counts), axis=1).astype(jnp.int32)
        xs = _gather_rows(slot_tok, block_used, h2.reshape(t, d // LANES, LANES), n_slots)
        hs = _expert_up(_expert_schedule(counts, n_blk_e, blk_start, ff // tf, n_blocks), xs, w_gu, b_gu4, l, tf)
        yb = _expert_down(_expert_schedule(counts, n_blk_e, blk_start, d // tn_d, n_blocks), hs, w_down, b_down4,
                          l, tn_d)
        x = _combine(dest, rows_c, yb, gates, x, mod, tm_c)

    y_prompt = _final_norm(x, final_norm, tm_n, 0, tp).reshape(bp, lp, d)
    y_sample = _final_norm(x, final_norm, tm_n, tp, ts).reshape(bs, ls, d)
    return (y_prompt, y_sample, jnp.stack(st_na_k, axis=1), jnp.stack(st_na_v, axis=1),
            jnp.stack(st_ga_k, axis=1), jnp.stack(st_ga_v, axis=1))
```

```python
import functools

import numpy as np
import jax
import jax.numpy as jnp
from jax import lax
from jax.experimental import pallas as pl
from jax.experimental.pallas import tpu as pltpu

F32 = jnp.float32
BF16 = jnp.bfloat16

LANES = 128
HEAD_DIM = 128
GRID_W = 64
NA_WIN_R = 8
NA_WIN_C = 16
N_KV_HEADS = 2
ROPE_THETA = 10000.0
CHUNK = 128
TOP_K = 4
SWIGLU_ALPHA = 1.702
SWIGLU_LIMIT = 7.0
MOE_BLOCK = 256
MOE_SUPER = 2 * MOE_BLOCK
NORM_EPS = 1e-6
NEG_INF = -1e30
ATT_SCALE = HEAD_DIM ** -0.5
NA_QROWS = 4
NA_KROWS = 12
VMEM_LIMIT = 56 * 1024 * 1024
SUBLANES = 8
ITEM_FILL, ITEM_COMPUTE, ITEM_NEW_WEIGHTS = 0, 1, 2
ISSUE_UNROLL = 8


def _cparams(sem):
    return pltpu.CompilerParams(dimension_semantics=sem, vmem_limit_bytes=VMEM_LIMIT)


def _pick(n, prefs):
    for p in prefs:
        if n % p == 0:
            return p
    return n


def _tile_mod_rows(tm, n_prompt, n_total, seq_sample):
    starts = np.arange(0, n_total, tm)
    return jnp.asarray(np.where(starts < n_prompt, 0, 1 + (starts - n_prompt) // seq_sample), jnp.int32)


def _rms(x):
    return x * lax.rsqrt(jnp.mean(x * x, axis=-1, keepdims=True) + NORM_EPS)


def _store_token_major(ref, val, first_row=0):
    rows, width = val.shape
    sub = width // LANES
    for s in range(sub):
        ref[pl.ds(first_row * sub + s, rows, stride=sub), :] = val[:, s * LANES:(s + 1) * LANES]


def _store_token_major_3d(ref, val, first_row):
    _store_token_major(ref.reshape(ref.shape[0] * ref.shape[1], LANES), val, first_row)


def _row_pitch(sub):
    return sub + SUBLANES if sub % (2 * SUBLANES) == 0 else sub


def _load_lane_chunk(buf, first_row, rows, pitch, s):
    return buf[pl.ds(first_row * pitch + s, rows, stride=pitch), :]


def _ada_kernel(cv_ref, w_ref, b_ref, o_ref):
    cv = cv_ref[...]
    a = (cv * jax.nn.sigmoid(cv)).astype(BF16)
    o_ref[0] = jnp.dot(a, w_ref[0].astype(BF16), preferred_element_type=F32) + b_ref[0]


def _ada_table(cv, w_ada, b_ada):
    depth, d, n = w_ada.shape
    tn = _pick(n, (512, 256, 128))
    return pl.pallas_call(
        _ada_kernel,
        grid=(depth, n // tn),
        in_specs=[pl.BlockSpec((8, d), lambda l, j: (0, 0)),
                  pl.BlockSpec((1, d, tn), lambda l, j: (l, 0, j)),
                  pl.BlockSpec((1, 1, tn), lambda l, j: (l, 0, j))],
        out_specs=pl.BlockSpec((1, 8, tn), lambda l, j: (l, 0, j)),
        out_shape=jax.ShapeDtypeStruct((depth, 8, n), F32),
        compiler_params=_cparams(("parallel", "parallel")),
        name="ada_table",
    )(cv, w_ada, b_ada.reshape(depth, 1, n))


def _part_ranges(parts, tm):
    edges = np.cumsum([0] + [p.shape[0] // tm for p in parts])
    return [(int(lo), int(hi)) for lo, hi in zip(edges[:-1], edges[1:])]


def _in_part(i, lo, hi):
    return jnp.logical_and(i >= lo, i < hi)


def _normmod_kernel(rows_ref, *refs, ranges):
    del rows_ref
    x_refs = refs[:len(ranges)]
    scale_ref, shift_ref, o_ref = refs[len(ranges):]
    i = pl.program_id(0)
    for (lo, hi), x_ref in zip(ranges, x_refs):
        @pl.when(_in_part(i, lo, hi))
        def _(x_ref=x_ref):
            o_ref[...] = (_rms(x_ref[...]) * (1.0 + scale_ref[0]) + shift_ref[0]).astype(o_ref.dtype)


def _normmod(parts, mod, rows, tm, which_scale, which_shift, out_dtype):
    d = parts[0].shape[1]
    ranges = _part_ranges(parts, tm)
    n_tiles = ranges[-1][1]
    x_specs = [pl.BlockSpec((tm, d), functools.partial(lambda i, r, lo, hi: (jnp.clip(i - lo, 0, hi - lo - 1), 0),
                                                       lo=lo, hi=hi)) for lo, hi in ranges]
    return pl.pallas_call(
        functools.partial(_normmod_kernel, ranges=ranges),
        grid_spec=pltpu.PrefetchScalarGridSpec(
            num_scalar_prefetch=1, grid=(n_tiles,),
            in_specs=x_specs + [pl.BlockSpec((1, 1, d), lambda i, r: (r[i] * 6 + which_scale, 0, 0)),
                                pl.BlockSpec((1, 1, d), lambda i, r: (r[i] * 6 + which_shift, 0, 0))],
            out_specs=pl.BlockSpec((tm, d), lambda i, r: (i, 0))),
        out_shape=jax.ShapeDtypeStruct((n_tiles * tm, d), out_dtype),
        compiler_params=_cparams(("parallel",)),
        name="normmod",
    )(rows, *parts, mod, mod)


def _mm_kernel(a_ref, w_ref, o_ref):
    o_ref[...] = jnp.dot(a_ref[...], w_ref[0], preferred_element_type=F32).astype(o_ref.dtype)


def _matmul(a, w, layer, out_dtype, tm, tn):
    m, k = a.shape
    n = w.shape[2]
    return pl.pallas_call(
        _mm_kernel,
        grid=(m // tm, n // tn),
        in_specs=[pl.BlockSpec((tm, k), lambda i, j: (i, 0)),
                  pl.BlockSpec((1, k, tn), lambda i, j: (layer, 0, j))],
        out_specs=pl.BlockSpec((tm, tn), lambda i, j: (i, j)),
        out_shape=jax.ShapeDtypeStruct((m, n), out_dtype),
        compiler_params=_cparams(("parallel", "parallel")),
        name="in_proj",
    )(a, w)


def _outproj_kernel(rows_ref, a_ref, w_ref, *refs, ranges):
    del rows_ref
    x_refs = refs[:len(ranges)]
    g_ref, o_ref = refs[len(ranges):]
    i = pl.program_id(0)
    update = g_ref[0] * jnp.dot(a_ref[...], w_ref[0], preferred_element_type=F32)
    for (lo, hi), x_ref in zip(ranges, x_refs):
        @pl.when(_in_part(i, lo, hi))
        def _(x_ref=x_ref):
            o_ref[...] = x_ref[...] + update


def _outproj(a, w, layer, x_parts, mod, rows, tm, tn):
    m, k = a.shape
    n = w.shape[2]
    ranges = _part_ranges(x_parts, tm)

    def x_map(i, j, r, lo, hi):
        return (jnp.clip(i - lo, 0, hi - lo - 1), jnp.where(_in_part(i, lo, hi), j, 0))

    x_specs = [pl.BlockSpec((tm, tn), functools.partial(x_map, lo=lo, hi=hi)) for lo, hi in ranges]
    return pl.pallas_call(
        functools.partial(_outproj_kernel, ranges=ranges),
        grid_spec=pltpu.PrefetchScalarGridSpec(
            num_scalar_prefetch=1, grid=(m // tm, n // tn),
            in_specs=[pl.BlockSpec((tm, k), lambda i, j, r: (i, 0)),
                      pl.BlockSpec((1, k, tn), lambda i, j, r: (layer, 0, j))] + x_specs
                     + [pl.BlockSpec((1, 1, tn), lambda i, j, r: (r[i] * 6 + 2, 0, j))],
            out_specs=pl.BlockSpec((tm, tn), lambda i, j, r: (i, j))),
        out_shape=jax.ShapeDtypeStruct((m, n), F32),
        compiler_params=_cparams(("parallel", "parallel")),
        name="out_proj",
    )(rows, a, w, *x_parts, mod)


def _gatebranch_kernel(h_ref, wg_ref, bg_ref, y_ref, wb_ref, o_ref, acc_ref):
    br = pl.program_id(2)
    g = jnp.dot(h_ref[...], wg_ref[0, 0], preferred_element_type=F32) + bg_ref[0]
    p = jnp.dot(y_ref[0], wb_ref[0, 0], preferred_element_type=F32)
    v = jax.nn.sigmoid(g) * p

    @pl.when(br == 0)
    def _():
        acc_ref[...] = v

    @pl.when(br > 0)
    def _():
        acc_ref[...] += v

    @pl.when(br == pl.num_programs(2) - 1)
    def _():
        o_ref[...] = acc_ref[...].astype(o_ref.dtype)


def _gatebranch(h, w_gate, b_gate, ys, w_branch, layer, tm, tn):
    t, d = h.shape
    _, nbr, bw, _ = w_branch.shape
    return pl.pallas_call(
        _gatebranch_kernel,
        grid=(t // tm, d // tn, nbr),
        in_specs=[pl.BlockSpec((tm, d), lambda i, j, b: (i, 0)),
                  pl.BlockSpec((1, 1, d, tn), lambda i, j, b: (layer, b, 0, j)),
                  pl.BlockSpec((1, 1, tn), lambda i, j, b: (b, 0, j)),
                  pl.BlockSpec((1, tm, bw), lambda i, j, b: (b, i, 0)),
                  pl.BlockSpec((1, 1, bw, tn), lambda i, j, b: (layer, b, 0, j))],
        out_specs=pl.BlockSpec((tm, tn), lambda i, j, b: (i, j)),
        out_shape=jax.ShapeDtypeStruct((t, d), BF16),
        scratch_shapes=[pltpu.VMEM((tm, tn), F32)],
        compiler_params=_cparams(("parallel", "parallel", "arbitrary")),
        name="gate_branch",
    )(h, w_gate, b_gate.reshape(nbr, 1, d), ys, w_branch)


def _softmax_parts(scores):
    m = scores[0].max(axis=-1, keepdims=True)
    for s in scores[1:]:
        m = jnp.maximum(m, s.max(axis=-1, keepdims=True))
    ps = [jnp.exp(s - m) for s in scores]
    l = ps[0].sum(axis=-1, keepdims=True)
    for p in ps[1:]:
        l = l + p.sum(axis=-1, keepdims=True)
    return ps, l


def _qk(q, k):
    return lax.dot_general(q, k, (((1,), (1,)), ((), ())), preferred_element_type=F32) * ATT_SCALE


def _store_head_state(ref, head, n_heads, val):
    ref[pl.ds(head, val.shape[0], stride=n_heads), :] = val


def _ctx_na_kernel(q_ref, k_ref, v_ref, o_ref, ks_ref, vs_ref, *, heads):
    k, v = k_ref[...], v_ref[...]
    _store_head_state(ks_ref, pl.program_id(1), heads, k)
    _store_head_state(vs_ref, pl.program_id(1), heads, v)
    s = _qk(q_ref[...].astype(BF16), k.astype(BF16))
    (p,), l = _softmax_parts([s])
    o_ref[...] = (jnp.dot(p.astype(BF16), v.astype(BF16), preferred_element_type=F32) / l).astype(o_ref.dtype)


def _ctx_na_attention(proj, nb, seq, heads, cq, ck, cv):
    state = jax.ShapeDtypeStruct((nb * seq * heads, HEAD_DIM), F32)
    state_spec = pl.BlockSpec((seq * heads, HEAD_DIM), lambda b, h: (b, 0))
    return pl.pallas_call(
        functools.partial(_ctx_na_kernel, heads=heads),
        grid=(nb, heads),
        in_specs=[pl.BlockSpec((seq, HEAD_DIM), lambda b, h: (b, cq + h)),
                  pl.BlockSpec((seq, HEAD_DIM), lambda b, h: (b, ck + h)),
                  pl.BlockSpec((seq, HEAD_DIM), lambda b, h: (b, cv + h))],
        out_specs=[pl.BlockSpec((seq, HEAD_DIM), lambda b, h: (b, h)), state_spec, state_spec],
        out_shape=[jax.ShapeDtypeStruct((nb * seq, heads * HEAD_DIM), BF16), state, state],
        compiler_params=_cparams(("parallel", "arbitrary")),
        name="ctx_na_attn",
    )(proj, proj, proj)


def _ctx_gqa_kernel(q_ref, k_ref, v_ref, qn_ref, kn_ref, o_ref, ks_ref, vs_ref, *, group):
    kv = pl.program_id(1) // group
    v = v_ref[...]
    kn = _rms(k_ref[...]) * kn_ref[...]
    _store_head_state(ks_ref, kv, N_KV_HEADS, kn)
    _store_head_state(vs_ref, kv, N_KV_HEADS, v)
    qn = _rms(q_ref[...]) * qn_ref[...]
    s = _qk(qn.astype(BF16), kn.astype(BF16))
    (p,), l = _softmax_parts([s])
    o_ref[...] = (jnp.dot(p.astype(BF16), v.astype(BF16), preferred_element_type=F32) / l).astype(o_ref.dtype)


def _ctx_gqa_attention(proj, q_norm, k_norm, nb, seq, heads, group, cq, ck, cv):
    state = jax.ShapeDtypeStruct((nb * seq * N_KV_HEADS, HEAD_DIM), F32)
    state_spec = pl.BlockSpec((seq * N_KV_HEADS, HEAD_DIM), lambda b, h: (b, 0))
    return pl.pallas_call(
        functools.partial(_ctx_gqa_kernel, group=group),
        grid=(nb, heads),
        in_specs=[pl.BlockSpec((seq, HEAD_DIM), lambda b, h: (b, cq + h)),
                  pl.BlockSpec((seq, HEAD_DIM), lambda b, h: (b, ck + h // group)),
                  pl.BlockSpec((seq, HEAD_DIM), lambda b, h: (b, cv + h // group)),
                  pl.BlockSpec((1, HEAD_DIM), lambda b, h: (0, 0)),
                  pl.BlockSpec((1, HEAD_DIM), lambda b, h: (0, 0))],
        out_specs=[pl.BlockSpec((seq, HEAD_DIM), lambda b, h: (b, h)), state_spec, state_spec],
        out_shape=[jax.ShapeDtypeStruct((nb * seq, heads * HEAD_DIM), BF16), state, state],
        compiler_params=_cparams(("parallel", "arbitrary")),
        name="ctx_gqa_attn",
    )(proj, proj, proj, q_norm.reshape(1, HEAD_DIM), k_norm.reshape(1, HEAD_DIM))


def _rope(x, cos_t, sin_t):
    return x * cos_t + pltpu.roll(x, HEAD_DIM // 2, 1) * sin_t


def _lat_gqa_kernel(q_ref, k_ref, v_ref, kc_ref, vc_ref, cos_ref, sin_ref, qn_ref, kn_ref, o_ref, kr_ref, *, tq):
    qi = pl.program_id(2)

    @pl.when(qi == 0)
    def _():
        kn = _rms(k_ref[...]) * kn_ref[...]
        kr_ref[...] = _rope(kn, cos_ref[...], sin_ref[...]).astype(BF16)

    rows = pl.ds(pl.multiple_of(qi * tq, tq), tq)
    qn = _rms(q_ref[...]) * qn_ref[...]
    qr = _rope(qn, cos_ref[rows, :], sin_ref[rows, :])
    s_ctx = _qk(qn.astype(BF16), kc_ref[0, 0].astype(BF16))
    s_lat = _qk(qr.astype(BF16), kr_ref[...])
    (p_ctx, p_lat), l = _softmax_parts([s_ctx, s_lat])
    out = (jnp.dot(p_ctx.astype(BF16), vc_ref[0, 0].astype(BF16), preferred_element_type=F32)
           + jnp.dot(p_lat.astype(BF16), v_ref[...].astype(BF16), preferred_element_type=F32))
    o_ref[...] = (out / l).astype(o_ref.dtype)


def _lat_gqa_attention(proj, cache_k, cache_v, layer, cos_t, sin_t, q_norm, k_norm,
                       row0, nb, seq, heads, group, cq, ck, cv):
    tq = _pick(seq, (512, 256, 128))
    nq = seq // tq
    rb = row0 // tq
    sb = row0 // seq
    past = cache_k.shape[2]
    return pl.pallas_call(
        functools.partial(_lat_gqa_kernel, tq=tq),
        grid=(nb, heads, nq),
        in_specs=[pl.BlockSpec((tq, HEAD_DIM), lambda b, h, i: (rb + b * nq + i, cq + h)),
                  pl.BlockSpec((seq, HEAD_DIM), lambda b, h, i: (sb + b, ck + h // group)),
                  pl.BlockSpec((seq, HEAD_DIM), lambda b, h, i: (sb + b, cv + h // group)),
                  pl.BlockSpec((1, 1, past, HEAD_DIM), lambda b, h, i: (b, layer, 0, h // group)),
                  pl.BlockSpec((1, 1, past, HEAD_DIM), lambda b, h, i: (b, layer, 0, h // group)),
                  pl.BlockSpec((seq, HEAD_DIM), lambda b, h, i: (0, 0)),
                  pl.BlockSpec((seq, HEAD_DIM), lambda b, h, i: (0, 0)),
                  pl.BlockSpec((1, HEAD_DIM), lambda b, h, i: (0, 0)),
                  pl.BlockSpec((1, HEAD_DIM), lambda b, h, i: (0, 0))],
        out_specs=pl.BlockSpec((tq, HEAD_DIM), lambda b, h, i: (b * nq + i, h)),
        out_shape=jax.ShapeDtypeStruct((nb * seq, heads * HEAD_DIM), BF16),
        scratch_shapes=[pltpu.VMEM((seq, HEAD_DIM), BF16)],
        compiler_params=_cparams(("parallel", "arbitrary", "arbitrary")),
        name="lat_gqa_attn",
    )(proj, proj, proj, cache_k, cache_v, cos_t, sin_t, q_norm.reshape(1, HEAD_DIM), k_norm.reshape(1, HEAD_DIM))


def _na_key_row0(i, n_rows):
    return jnp.clip(NA_QROWS * i - NA_WIN_R // 2, 0, n_rows - NA_KROWS)


def _lat_na_kernel(q_ref, k_ref, v_ref, kc_ref, vc_ref, bias_ref, o_ref, *, n_rows):
    i = pl.program_id(2)
    start = pl.multiple_of(_na_key_row0(i, n_rows) * GRID_W, GRID_W)
    keys = pl.ds(start, NA_KROWS * GRID_W)
    q = q_ref[...].astype(BF16)
    s_ctx = _qk(q, kc_ref[0, 0].astype(BF16))
    s_win = _qk(q, k_ref[keys, :].astype(BF16)) + bias_ref[0, 0, 0]
    (p_ctx, p_win), l = _softmax_parts([s_ctx, s_win])
    out = (jnp.dot(p_ctx.astype(BF16), vc_ref[0, 0].astype(BF16), preferred_element_type=F32)
           + jnp.dot(p_win.astype(BF16), v_ref[keys, :].astype(BF16), preferred_element_type=F32))
    o_ref[...] = (out / l).astype(o_ref.dtype)


def _na_bias_tables(rpb, n_rows):
    n_layers, heads = rpb.shape[:2]
    n_blocks = n_rows // NA_QROWS
    pad = GRID_W - NA_WIN_C
    padded = jnp.pad(rpb.astype(F32), ((0, 0), (0, 0), (0, 0), (pad, pad)))
    ccol = jnp.stack([padded[..., GRID_W - 1 - qc:2 * GRID_W - 1 - qc] for qc in range(GRID_W)], axis=-2)
    qc = np.arange(GRID_W)[:, None]
    kc = np.arange(GRID_W)[None, :]
    cs = np.clip(qc - NA_WIN_C // 2, 0, GRID_W - NA_WIN_C)
    ccol = jnp.where((kc >= cs) & (kc < cs + NA_WIN_C), ccol, NEG_INF)
    masked = jnp.full((n_layers, heads, GRID_W, GRID_W), NEG_INF, F32)
    kinds = []
    for i in (0, min(2, n_blocks - 1), n_blocks - 1):
        k0 = int(np.clip(NA_QROWS * i - NA_WIN_R // 2, 0, n_rows - NA_KROWS))
        q_rows = []
        for qr in range(NA_QROWS * i, NA_QROWS * (i + 1)):
            rs = int(np.clip(qr - NA_WIN_R // 2, 0, n_rows - NA_WIN_R))
            q_rows.append(jnp.concatenate(
                [ccol[:, :, kr - qr + NA_WIN_R - 1] if rs <= kr < rs + NA_WIN_R else masked
                 for kr in range(k0, k0 + NA_KROWS)], axis=-1))
        kinds.append(jnp.concatenate(q_rows, axis=-2))
    return jnp.stack(kinds, axis=2)


def _lat_na_attention(proj, cache_k, cache_v, layer, bias, row0, nb, seq, heads, cq, ck, cv):
    n_rows = seq // GRID_W
    tq = NA_QROWS * GRID_W
    nq = seq // tq
    rb = row0 // tq
    sb = row0 // seq
    past = cache_k.shape[2]
    kw = NA_KROWS * GRID_W

    def bias_map(b, h, i):
        return (layer, h, jnp.where(i == 0, 0, jnp.where(i == nq - 1, 2, 1)), 0, 0)

    return pl.pallas_call(
        functools.partial(_lat_na_kernel, n_rows=n_rows),
        grid=(nb, heads, nq),
        in_specs=[pl.BlockSpec((tq, HEAD_DIM), lambda b, h, i: (rb + b * nq + i, cq + h)),
                  pl.BlockSpec((seq, HEAD_DIM), lambda b, h, i: (sb + b, ck + h)),
                  pl.BlockSpec((seq, HEAD_DIM), lambda b, h, i: (sb + b, cv + h)),
                  pl.BlockSpec((1, 1, past, HEAD_DIM), lambda b, h, i: (b, layer, 0, h)),
                  pl.BlockSpec((1, 1, past, HEAD_DIM), lambda b, h, i: (b, layer, 0, h)),
                  pl.BlockSpec((1, 1, 1, tq, kw), bias_map)],
        out_specs=pl.BlockSpec((tq, HEAD_DIM), lambda b, h, i: (b * nq + i, h)),
        out_shape=jax.ShapeDtypeStruct((nb * seq, heads * HEAD_DIM), BF16),
        compiler_params=_cparams(("parallel", "parallel", "arbitrary")),
        name="lat_na_attn",
    )(proj, proj, proj, cache_k, cache_v, bias)


def _conv_kernel(b_ref, c_ref, x_ref, w_ref, o_ref):
    p = c_ref[...] * x_ref[...]
    n = p.shape[0]
    row = lax.broadcasted_iota(jnp.int32, p.shape, 0)
    prev = jnp.where(row == 0, 0.0, pltpu.roll(p, 1, 0))
    nxt = jnp.where(row == n - 1, 0.0, pltpu.roll(p, n - 1, 0))
    w = w_ref[...]
    o_ref[...] = (b_ref[...] * (prev * w[0:1] + p * w[1:2] + nxt * w[2:3])).astype(o_ref.dtype)


def _short_conv(proj, conv_w, row0, nb, seq, width, off_b, off_c, off_x):
    tc = _pick(width, (256, 128))
    assert off_b % tc == 0 and off_c % tc == 0 and off_x % tc == 0
    sb = row0 // seq
    k = width // tc
    cb, cc, cx = off_b // tc, off_c // tc, off_x // tc
    return pl.pallas_call(
        _conv_kernel,
        grid=(nb, k),
        in_specs=[pl.BlockSpec((seq, tc), lambda b, j: (sb + b, cb + j)),
                  pl.BlockSpec((seq, tc), lambda b, j: (sb + b, cc + j)),
                  pl.BlockSpec((seq, tc), lambda b, j: (sb + b, cx + j)),
                  pl.BlockSpec((3, tc), lambda b, j: (0, j))],
        out_specs=pl.BlockSpec((seq, tc), lambda b, j: (b, j)),
        out_shape=jax.ShapeDtypeStruct((nb * seq, width), BF16),
        compiler_params=_cparams(("parallel", "parallel")),
        name="short_conv",
    )(proj, proj, proj, conv_w)


def _sgu_kernel(*refs, pieces):
    u_refs, v_refs = refs[:pieces], refs[pieces:2 * pieces]
    g_ref, b_ref, w_ref, bt_ref, o_ref = refs[2 * pieces:]
    rows, width = o_ref.shape
    groups = w_ref.shape[0]
    gw = width // groups
    pw = width // pieces
    for c in range(rows // CHUNK):
        rs = slice(c * CHUNK, (c + 1) * CHUNK)
        v = jnp.concatenate([r[rs, :] for r in v_refs], axis=-1)
        mu = jnp.mean(v, axis=-1, keepdims=True)
        var = jnp.mean(jnp.square(v - mu), axis=-1, keepdims=True)
        vn = ((v - mu) * lax.rsqrt(var + NORM_EPS)) * g_ref[...] + b_ref[...]
        for g in range(groups):
            cs = slice(g * gw, (g + 1) * gw)
            mixed = jnp.dot(w_ref[g].astype(BF16), vn[:, cs].astype(BF16), preferred_element_type=F32)
            mixed = mixed + bt_ref[:, g:g + 1]
            piece, lo = divmod(g * gw, pw)
            o_ref[rs, cs] = (u_refs[piece][rs, lo:lo + gw] * mixed).astype(o_ref.dtype)


def _spatial_gating(proj, ln_g, ln_b, w_s, b_s, width, off_u, off_v):
    t = proj.shape[0]
    rows = _pick(t, (256, 128))
    groups = w_s.shape[0]
    pw = int(np.gcd.reduce([width, off_u, off_v]))
    pieces = width // pw
    assert pw % (width // groups) == 0
    col_specs = [pl.BlockSpec((rows, pw), functools.partial(lambda i, c: (i, c), c=off // pw + p))
                 for off in (off_u, off_v) for p in range(pieces)]
    return pl.pallas_call(
        functools.partial(_sgu_kernel, pieces=pieces),
        grid=(t // rows,),
        in_specs=col_specs + [pl.BlockSpec((1, width), lambda i: (0, 0)),
                              pl.BlockSpec((1, width), lambda i: (0, 0)),
                              pl.BlockSpec((groups, CHUNK, CHUNK), lambda i: (0, 0, 0)),
                              pl.BlockSpec((CHUNK, groups), lambda i: (0, 0))],
        out_specs=pl.BlockSpec((rows, width), lambda i: (i, 0)),
        out_shape=jax.ShapeDtypeStruct((t, width), BF16),
        compiler_params=_cparams(("parallel",)),
        name="spatial_gating",
    )(*([proj] * (2 * pieces)), ln_g.reshape(1, width), ln_b.reshape(1, width), w_s, b_s.T)


def _router_kernel(rows_ref, x_ref, scale_ref, shift_ref, wr_ref, br_ref,
                   h_ref, idx_ref, gate_ref, rank_ref, cnt_ref, carry_ref):
    del rows_ref
    i = pl.program_id(0)

    @pl.when(i == 0)
    def _():
        carry_ref[...] = jnp.zeros_like(carry_ref)

    h = _rms(x_ref[...]) * (1.0 + scale_ref[0]) + shift_ref[0]
    _store_token_major(h_ref, h)
    tm = h.shape[0]
    n_exp = wr_ref.shape[1]
    w = wr_ref[...]
    h_hi = h.astype(BF16)
    h_lo = (h - h_hi.astype(F32)).astype(BF16)
    w_hi = w.astype(BF16)
    w_lo = (w - w_hi.astype(F32)).astype(BF16)
    both = jnp.dot(h_hi, jnp.concatenate([w_hi, w_lo], axis=1), preferred_element_type=F32)
    logits = (both[:, :n_exp] + both[:, n_exp:]
              + jnp.dot(h_lo, w_hi, preferred_element_type=F32) + br_ref[...])
    lane_e = lax.broadcasted_iota(jnp.int32, (tm, n_exp), 1)
    lane_o = lax.broadcasted_iota(jnp.int32, (tm, LANES), 1)

    work = logits
    vals, sels = [], []
    for _ in range(TOP_K):
        m = work.max(axis=-1, keepdims=True)
        sel = jnp.min(jnp.where(work == m, lane_e, n_exp), axis=-1, keepdims=True)
        vals.append(m)
        sels.append(sel)
        work = jnp.where(lane_e == sel, -jnp.inf, work)
    exps = [jnp.exp(v - vals[0]) for v in vals]
    denom = exps[0]
    for e in exps[1:]:
        denom = denom + e

    onehot = jnp.zeros((tm, n_exp), F32)
    for sel in sels:
        onehot = onehot + (lane_e == sel).astype(F32)
    r = lax.broadcasted_iota(jnp.int32, (tm, tm), 0)
    c = lax.broadcasted_iota(jnp.int32, (tm, tm), 1)
    before = (r > c).astype(BF16)
    prefix = jnp.dot(before, onehot.astype(BF16), preferred_element_type=F32) + carry_ref[...]
    carry = carry_ref[...] + onehot.sum(axis=0, keepdims=True)
    carry_ref[...] = carry
    cnt_ref[...] = carry

    idx_out = jnp.zeros((tm, LANES), jnp.int32)
    gate_out = jnp.zeros((tm, LANES), F32)
    rank_out = jnp.zeros((tm, LANES), jnp.int32)
    for k in range(TOP_K):
        rank = jnp.sum(jnp.where(lane_e == sels[k], prefix, 0.0), axis=-1, keepdims=True).astype(jnp.int32)
        idx_out = jnp.where(lane_o == k, sels[k], idx_out)
        gate_out = jnp.where(lane_o == k, exps[k] / denom, gate_out)
        rank_out = jnp.where(lane_o == k, rank, rank_out)
    idx_ref[...] = idx_out
    gate_ref[...] = gate_out
    rank_ref[...] = rank_out


def _router(x, mod, rows, tm, w_router, b_router):
    t, d = x.shape
    n_exp = w_router.shape[1]
    return pl.pallas_call(
        _router_kernel,
        grid_spec=pltpu.PrefetchScalarGridSpec(
            num_scalar_prefetch=1, grid=(t // tm,),
            in_specs=[pl.BlockSpec((tm, d), lambda i, r: (i, 0)),
                      pl.BlockSpec((1, 1, d), lambda i, r: (r[i] * 6 + 4, 0, 0)),
                      pl.BlockSpec((1, 1, d), lambda i, r: (r[i] * 6 + 3, 0, 0)),
                      pl.BlockSpec((d, n_exp), lambda i, r: (0, 0)),
                      pl.BlockSpec((1, n_exp), lambda i, r: (0, 0))],
            out_specs=[pl.BlockSpec((tm * d // LANES, LANES), lambda i, r: (i, 0)),
                       pl.BlockSpec((tm, LANES), lambda i, r: (i, 0)),
                       pl.BlockSpec((tm, LANES), lambda i, r: (i, 0)),
                       pl.BlockSpec((tm, LANES), lambda i, r: (i, 0)),
                       pl.BlockSpec((1, n_exp), lambda i, r: (0, 0))],
            scratch_shapes=[pltpu.VMEM((1, n_exp), F32)]),
        out_shape=[jax.ShapeDtypeStruct((t * d // LANES, LANES), F32),
                   jax.ShapeDtypeStruct((t, LANES), jnp.int32),
                   jax.ShapeDtypeStruct((t, LANES), F32),
                   jax.ShapeDtypeStruct((t, LANES), jnp.int32),
                   jax.ShapeDtypeStruct((1, n_exp), F32)],
        compiler_params=_cparams(("arbitrary",)),
        name="router",
    )(rows, x, mod, mod, w_router, b_router.reshape(1, n_exp))


def _wait_rows(buf_ref, n_rows, sub, sem):
    done = buf_ref.at[pl.ds(0, n_rows * sub), :]
    pltpu.make_async_copy(done, done, sem).wait()


def _gather_kernel(tok_ref, used_ref, src_ref, o_ref, buf0, buf1, sems):
    i = pl.program_id(0)
    last = pl.num_programs(0) - 1
    n_rows = o_ref.shape[0]
    sub = src_ref.shape[1]
    pitch = _row_pitch(sub)
    bufs = (buf0, buf1)

    def issue(block, slot):
        def body(g, carry):
            for j in range(ISSUE_UNROLL):
                r = g * ISSUE_UNROLL + j
                rows = pl.ds(pl.multiple_of(r * pitch, SUBLANES), sub)
                copy = pltpu.make_async_copy(src_ref.at[tok_ref[block * n_rows + r]], bufs[slot].at[rows, :], sems.at[slot])
                copy.start(priority=j % 2)
            return carry
        lax.fori_loop(0, n_rows // ISSUE_UNROLL, body, 0)

    for slot in (0, 1):
        @pl.when(i % 2 == slot)
        def _(slot=slot):
            if slot == 0:
                @pl.when(jnp.logical_and(i == 0, used_ref[0] == 1))
                def _():
                    issue(0, 0)

            @pl.when(jnp.logical_and(i < last, used_ref[jnp.minimum(i + 1, last)] == 1))
            def _():
                issue(i + 1, 1 - slot)

            @pl.when(used_ref[i] == 1)
            def _():
                _wait_rows(bufs[slot], n_rows, sub, sems.at[slot])
                for s in range(sub):
                    chunk = _load_lane_chunk(bufs[slot], 0, n_rows, pitch, s)
                    o_ref[:, s * LANES:(s + 1) * LANES] = chunk.astype(o_ref.dtype)

    @pl.when(used_ref[i] == 0)
    def _():
        o_ref[...] = jnp.zeros_like(o_ref)


def _gather_rows(slot_tok, block_used, src3, n_slots):
    _, sub, _ = src3.shape
    d = sub * LANES
    buf = pltpu.VMEM((MOE_BLOCK * _row_pitch(sub), LANES), F32)
    return pl.pallas_call(
        _gather_kernel,
        grid_spec=pltpu.PrefetchScalarGridSpec(
            num_scalar_prefetch=2, grid=(n_slots // MOE_BLOCK,),
            in_specs=[pl.BlockSpec(memory_space=pl.ANY)],
            out_specs=pl.BlockSpec((MOE_BLOCK, d), lambda i, t, u: (i, 0)),
            scratch_shapes=[buf, buf, pltpu.SemaphoreType.DMA((2,))]),
        out_shape=jax.ShapeDtypeStruct((n_slots, d), BF16),
        compiler_params=_cparams(("arbitrary",)),
        name="expert_gather",
    )(slot_tok, block_used, src3)


N_SCHED = 8


def _stream_weights(sched, tile_copies, stage, wbf_refs):
    e_ref, t_ref, flag_ref, ne_ref, nt_ref = sched[0], sched[1], sched[4], sched[6], sched[7]
    w = pl.program_id(0)

    @pl.when(flag_ref[w] == ITEM_NEW_WEIGHTS)
    def _():
        @pl.when(w == 0)
        def _():
            for c in tile_copies(e_ref[0], t_ref[0]):
                c.start()

        for c in tile_copies(e_ref[w], t_ref[w]):
            c.wait()
        for p, ref in enumerate(wbf_refs):
            ref[...] = stage[p].astype(BF16)

        @pl.when(ne_ref[w] >= 0)
        def _():
            for c in tile_copies(ne_ref[w], nt_ref[w]):
                c.start()


def _run_item(sched, in_ref, compute, write):
    w = pl.program_id(0)
    real = sched[4][w] != ITEM_FILL
    first_only = sched[5][w] == 1
    half = in_ref.shape[0] // 2

    @pl.when(jnp.logical_and(real, jnp.logical_not(first_only)))
    def _():
        y = compute(in_ref[...])
        write(0, y[:half])
        write(1, y[half:])

    @pl.when(jnp.logical_and(real, first_only))
    def _():
        write(0, compute(in_ref[:half, :]))
        write(1, None)

    @pl.when(jnp.logical_not(real))
    def _():
        write(0, None)
        write(1, None)


def _expert_up_kernel(*refs, layer, tf, nf):
    sched = refs[:N_SCHED]
    x_ref, w_hbm, bg_ref, bu_ref, o_ref, stage, wg_bf, wu_bf, sems = refs[N_SCHED:]
    half = o_ref.shape[0] // 2

    def tile_copies(e, t):
        cols_g = pl.ds(pl.multiple_of(t * tf, tf), tf)
        cols_u = pl.ds(pl.multiple_of((nf + t) * tf, tf), tf)
        return (pltpu.make_async_copy(w_hbm.at[layer, e, :, cols_g], stage.at[0], sems.at[0]),
                pltpu.make_async_copy(w_hbm.at[layer, e, :, cols_u], stage.at[1], sems.at[1]))

    _stream_weights(sched, tile_copies, stage, (wg_bf, wu_bf))

    def compute(x):
        g = jnp.dot(x, wg_bf[...], preferred_element_type=F32) + bg_ref[0, 0]
        u = jnp.dot(x, wu_bf[...], preferred_element_type=F32) + bu_ref[0, 0]
        g = jnp.minimum(g, SWIGLU_LIMIT)
        u = jnp.clip(u, -SWIGLU_LIMIT, SWIGLU_LIMIT)
        return (g * jax.nn.sigmoid(SWIGLU_ALPHA * g) * (u + 1.0)).astype(BF16)

    def write(which, val):
        rows = pl.ds(which * half, half)
        o_ref[rows, :] = jnp.zeros((half, o_ref.shape[1]), o_ref.dtype) if val is None else val

    _run_item(sched, x_ref, compute, write)


def _expert_up(sched, xs, w_gu, b_gu4, layer, tf):
    n_slots, d = xs.shape
    ff = w_gu.shape[3] // 2
    nf = ff // tf
    n_items = sched[0].shape[0]
    return pl.pallas_call(
        functools.partial(_expert_up_kernel, layer=layer, tf=tf, nf=nf),
        grid_spec=pltpu.PrefetchScalarGridSpec(
            num_scalar_prefetch=N_SCHED, grid=(n_items,),
            in_specs=[pl.BlockSpec((MOE_SUPER, d), lambda w, e, t, b, *_: (b[w], 0)),
                      pl.BlockSpec(memory_space=pl.ANY),
                      pl.BlockSpec((1, 1, 1, tf), lambda w, e, t, *_: (layer, e[w], 0, t[w])),
                      pl.BlockSpec((1, 1, 1, tf), lambda w, e, t, *_: (layer, e[w], 0, nf + t[w]))],
            out_specs=pl.BlockSpec((MOE_SUPER, tf), lambda w, e, t, b, o, *_: (b[w], o[w])),
            scratch_shapes=[pltpu.VMEM((2, d, tf), F32), pltpu.VMEM((d, tf), BF16), pltpu.VMEM((d, tf), BF16),
                            pltpu.SemaphoreType.DMA((2,))]),
        out_shape=jax.ShapeDtypeStruct((n_slots, ff), BF16),
        compiler_params=_cparams(("arbitrary",)),
        name="expert_up",
    )(*sched, xs, w_gu, b_gu4, b_gu4)


def _expert_down_kernel(*refs, layer, tn):
    sched = refs[:N_SCHED]
    h_ref, w_hbm, bd_ref, o_ref, stage, wd_bf, sems = refs[N_SCHED:]
    half = h_ref.shape[0] // 2

    def tile_copies(e, t):
        cols = pl.ds(pl.multiple_of(t * tn, tn), tn)
        return (pltpu.make_async_copy(w_hbm.at[layer, e, :, cols], stage.at[0], sems.at[0]),)

    _stream_weights(sched, tile_copies, stage, (wd_bf,))

    def compute(h):
        return jnp.dot(h, wd_bf[...], preferred_element_type=F32) + bd_ref[0, 0]

    def write(which, val):
        if val is None:
            o_ref[pl.ds(which * half, half), :, :] = jnp.zeros((half,) + o_ref.shape[1:], o_ref.dtype)
        else:
            _store_token_major_3d(o_ref, val, which * half)

    _run_item(sched, h_ref, compute, write)


def _expert_down(sched, hs, w_down, b_down4, layer, tn):
    n_slots, ff = hs.shape
    d = w_down.shape[3]
    n_items = sched[0].shape[0]
    return pl.pallas_call(
        functools.partial(_expert_down_kernel, layer=layer, tn=tn),
        grid_spec=pltpu.PrefetchScalarGridSpec(
            num_scalar_prefetch=N_SCHED, grid=(n_items,),
            in_specs=[pl.BlockSpec((MOE_SUPER, ff), lambda w, e, t, b, *_: (b[w], 0)),
                      pl.BlockSpec(memory_space=pl.ANY),
                      pl.BlockSpec((1, 1, 1, tn), lambda w, e, t, *_: (layer, e[w], 0, t[w]))],
            out_specs=pl.BlockSpec((MOE_SUPER, tn // LANES, LANES), lambda w, e, t, b, o, *_: (b[w], o[w], 0)),
            scratch_shapes=[pltpu.VMEM((1, ff, tn), F32), pltpu.VMEM((ff, tn), BF16),
                            pltpu.SemaphoreType.DMA((1,))]),
        out_shape=jax.ShapeDtypeStruct((n_slots, d // LANES, LANES), F32),
        compiler_params=_cparams(("arbitrary",)),
        name="expert_down",
    )(*sched, hs, w_down, b_down4)


def _expert_schedule(counts, n_blk_e, blk_start, n_tiles, n_blocks):
    i32 = jnp.int32
    n_exp = n_blk_e.shape[0]
    items = n_blk_e * n_tiles
    item_end = jnp.cumsum(items)
    n_used = jnp.sum(n_blk_e)
    total = n_used * n_tiles
    w = jnp.arange(n_tiles * n_blocks, dtype=i32)
    wc = jnp.clip(w, 0, jnp.maximum(total - 1, 0))
    e = jnp.minimum(jnp.sum(item_end[None, :] <= wc[:, None], axis=1), n_exp - 1).astype(i32)
    of_e = e[:, None] == jnp.arange(n_exp, dtype=i32)[None, :]

    def lookup(table):
        return jnp.sum(jnp.where(of_e, table[None, :], 0), axis=1)

    local = wc - lookup(item_end - items)
    nbe = jnp.maximum(lookup(n_blk_e), 1)
    tile = local // nbe
    blk_start = lookup(blk_start)
    valid = w < total
    spare = jnp.maximum(w - total, 0)
    n_unused = jnp.maximum(n_blocks - n_used, 1)
    blk = jnp.where(valid, blk_start + local % nbe, n_used + spare % n_unused)
    out_tile = jnp.where(valid, tile, spare // n_unused)
    flag = jnp.where(valid, jnp.where(local % nbe == 0, ITEM_NEW_WEIGHTS, ITEM_COMPUTE), ITEM_FILL)
    first_only = valid & (lookup(counts) - (local % nbe) * MOE_SUPER <= MOE_SUPER // 2)
    experts = jnp.arange(n_exp, dtype=i32)
    nonempty = n_blk_e > 0
    before = jnp.cumsum(nonempty) - nonempty
    later = nonempty[None, :] & (experts[None, :] > experts[:, None])
    next_nonempty = jnp.min(jnp.where(later, experts[None, :], n_exp), axis=1)
    group = lookup(before) * n_tiles + tile
    last_tile = tile + 1 >= n_tiles
    next_e = jnp.where(last_tile, lookup(next_nonempty), e)
    next_e = jnp.where(group + 1 < jnp.sum(nonempty) * n_tiles, next_e, -1)
    next_tile = jnp.where(last_tile, 0, tile + 1)
    return tuple(a.astype(i32) for a in (e, tile, blk, out_tile, flag, first_only, next_e, next_tile))


def _combine_kernel(dest_ref, rows_ref, y_ref, gate_ref, x_ref, g2_ref, o_ref, buf0, buf1, sems):
    del rows_ref
    bufs = (buf0, buf1)
    i = pl.program_id(0)
    tm = x_ref.shape[0]
    sub = y_ref.shape[1]
    pitch = _row_pitch(sub)

    def issue(tile, slot):
        def body(g, carry):
            for u in range(ISSUE_UNROLL):
                p = g * ISSUE_UNROLL + u
                k = p // tm
                t = p - k * tm
                row = dest_ref[(tile * tm + t) * TOP_K + k]
                rows = pl.ds(pl.multiple_of(p * pitch, SUBLANES), sub)
                copy = pltpu.make_async_copy(y_ref.at[row], bufs[slot].at[rows, :], sems.at[slot])
                copy.start(priority=u % 2)
            return carry
        lax.fori_loop(0, TOP_K * tm // ISSUE_UNROLL, body, 0)

    for slot in (0, 1):
        @pl.when(i % 2 == slot)
        def _(slot=slot):
            if slot == 0:
                @pl.when(i == 0)
                def _():
                    issue(0, 0)

            @pl.when(i + 1 < pl.num_programs(0))
            def _():
                issue(i + 1, 1 - slot)

            gate = gate_ref[...]
            buf = bufs[slot]
            _wait_rows(buf, TOP_K * tm, sub, sems.at[slot])
            for s in range(sub):
                cols = slice(s * LANES, (s + 1) * LANES)
                acc = gate[:, 0:1] * _load_lane_chunk(buf, 0, tm, pitch, s)
                for k in range(1, TOP_K):
                    acc = acc + gate[:, k:k + 1] * _load_lane_chunk(buf, k * tm, tm, pitch, s)
                o_ref[:, cols] = x_ref[:, cols] + g2_ref[0, :, cols] * acc


def _combine(dest_flat, rows, y3, gates, x, mod, tm):
    t, d = x.shape
    sub = y3.shape[1]
    buf = pltpu.VMEM((TOP_K * tm * _row_pitch(sub), LANES), F32)
    return pl.pallas_call(
        _combine_kernel,
        grid_spec=pltpu.PrefetchScalarGridSpec(
            num_scalar_prefetch=2, grid=(t // tm,),
            in_specs=[pl.BlockSpec(memory_space=pl.ANY),
                      pl.BlockSpec((tm, LANES), lambda i, dst, r: (i, 0)),
                      pl.BlockSpec((tm, d), lambda i, dst, r: (i, 0)),
                      pl.BlockSpec((1, 1, d), lambda i, dst, r: (r[i] * 6 + 5, 0, 0))],
            out_specs=pl.BlockSpec((tm, d), lambda i, dst, r: (i, 0)),
            scratch_shapes=[buf, buf, pltpu.SemaphoreType.DMA((2,))]),
        out_shape=jax.ShapeDtypeStruct((t, d), F32),
        compiler_params=_cparams(("arbitrary",)),
        name="expert_combine",
    )(dest_flat, rows, y3, gates, x, mod)


def _final_kernel(x_ref, w_ref, o_ref):
    o_ref[...] = _rms(x_ref[...]) * w_ref[...]


def _final_norm(x, w, tm, row0, n_rows):
    d = x.shape[1]
    first = row0 // tm
    return pl.pallas_call(
        _final_kernel,
        grid=(n_rows // tm,),
        in_specs=[pl.BlockSpec((tm, d), lambda i: (first + i, 0)), pl.BlockSpec((1, d), lambda i: (0, 0))],
        out_specs=pl.BlockSpec((tm, d), lambda i: (i, 0)),
        out_shape=jax.ShapeDtypeStruct((n_rows, d), F32),
        compiler_params=_cparams(("parallel",)),
        name="final_norm",
    )(x, w.reshape(1, d))


def _rope_tables(length):
    pairs = HEAD_DIM // 4
    t = np.arange(length)
    row = (t // GRID_W).astype(np.float32)
    col = (t % GRID_W).astype(np.float32)
    inv = jnp.asarray(ROPE_THETA, F32) ** (-jnp.arange(pairs, dtype=F32) / pairs)
    ang = jnp.concatenate([jnp.asarray(row)[:, None] * inv, jnp.asarray(col)[:, None] * inv], axis=-1)
    cos, sin = jnp.cos(ang), jnp.sin(ang)
    return jnp.concatenate([cos, cos], axis=-1), jnp.concatenate([-sin, sin], axis=-1)


def kernel(x_prompt, x_sample, c, cache_na_k, cache_na_v, cache_gqa_k, cache_gqa_v, c_ctx, w_ada, b_ada, w_in, q_norm, k_norm, na_rpb, conv_w, sgu_w, sgu_b, sgu_ln_g, sgu_ln_b, w_branch, w_gate, b_gate, w_out, w_router, b_router, w_gu, b_gu, w_down, b_down, final_norm):
    bp, lp, d = x_prompt.shape
    bs, ls, _ = x_sample.shape
    depth = w_ada.shape[0]
    past = cache_na_k.shape[2]
    tp, ts = bp * lp, bs * ls
    t = tp + ts
    bw = w_branch.shape[2]
    heads = bw // HEAD_DIM
    group = heads // N_KV_HEADS
    n_exp = w_router.shape[2]
    ff = w_down.shape[2]
    n_rows = ls // GRID_W
    assert bw % LANES == 0 and sgu_w.shape[1] * CHUNK == bw and sgu_w.shape[2] == CHUNK

    widths = (bw, bw, bw, heads * HEAD_DIM, N_KV_HEADS * HEAD_DIM, N_KV_HEADS * HEAD_DIM, bw, bw, bw, bw, bw)
    offs = np.concatenate([[0], np.cumsum(widths)])
    blk = [int(o) // LANES for o in offs]
    offs = [int(o) for o in offs]
    assert tp % ls == 0

    x_parts = (x_prompt.reshape(tp, d), x_sample.reshape(ts, d))
    cv = jnp.concatenate([c_ctx[None, :], c, jnp.zeros((8 - 1 - bs, d), F32)], axis=0)
    mod_all = _ada_table(cv, w_ada, b_ada)
    cos_t, sin_t = _rope_tables(ls)
    bias = _na_bias_tables(na_rpb, n_rows)
    cna_k = cache_na_k.reshape(bs, depth, past, heads * HEAD_DIM)
    cna_v = cache_na_v.reshape(bs, depth, past, heads * HEAD_DIM)
    cga_k = cache_gqa_k.reshape(bs, depth, past, N_KV_HEADS * HEAD_DIM)
    cga_v = cache_gqa_v.reshape(bs, depth, past, N_KV_HEADS * HEAD_DIM)

    tm_n = _pick(lp, (256, 128))
    tm_m = _pick(ls, (1024, 512, 256))
    tm_m = tm_m if tp % tm_m == 0 else tm_n
    rows_n = _tile_mod_rows(tm_n, tp, t, ls)
    rows_m = _tile_mod_rows(tm_m, tp, t, ls)
    tm_c = CHUNK
    rows_c = _tile_mod_rows(tm_c, tp, t, ls)
    tn = _pick(d, (512, 256, 128))
    tn_in = _pick(w_in.shape[2], (512, 256, 128))
    tf = _pick(ff, (512, 256, 128))
    tn_d = _pick(d, (2048, 1024, 512, 256, 128))
    n_assign = t * TOP_K
    n_blocks = -(-n_assign // MOE_SUPER) + n_exp
    n_slots = n_blocks * MOE_SUPER
    b_gu4 = b_gu.reshape(depth, n_exp, 1, 2 * ff)
    b_down4 = b_down.reshape(depth, n_exp, 1, d)
    w_in_bf, w_gate_bf, w_branch_bf, w_out_bf = (w.astype(BF16) for w in (w_in, w_gate, w_branch, w_out))

    st_na_k, st_na_v, st_ga_k, st_ga_v = [], [], [], []
    for l in range(depth):
        mod = mod_all[l].reshape(8 * 6, 1, d)
        h = _normmod(x_parts, mod, rows_n, tm_n, 1, 0, BF16)
        proj = _matmul(h, w_in_bf, l, F32, tm_m, tn_in)

        y_na_p, na_k_p, na_v_p = _ctx_na_attention(proj, bp, lp, heads, blk[0], blk[1], blk[2])
        y_ga_p, ga_k_p, ga_v_p = _ctx_gqa_attention(proj, q_norm[l], k_norm[l], bp, lp, heads, group,
                                                    blk[3], blk[4], blk[5])
        st_na_k.append(na_k_p.reshape(bp, lp, heads, HEAD_DIM))
        st_na_v.append(na_v_p.reshape(bp, lp, heads, HEAD_DIM))
        st_ga_k.append(ga_k_p.reshape(bp, lp, N_KV_HEADS, HEAD_DIM))
        st_ga_v.append(ga_v_p.reshape(bp, lp, N_KV_HEADS, HEAD_DIM))

        y_na_s = _lat_na_attention(proj, cna_k, cna_v, l, bias, tp, bs, ls, heads, blk[0], blk[1], blk[2])
        y_ga_s = _lat_gqa_attention(proj, cga_k, cga_v, l, cos_t, sin_t, q_norm[l], k_norm[l],
                                    tp, bs, ls, heads, group, blk[3], blk[4], blk[5])

        y_sc = jnp.concatenate([
            _short_conv(proj, conv_w[l], 0, bp, lp, bw, offs[6], offs[7], offs[8]),
            _short_conv(proj, conv_w[l], tp, bs, ls, bw, offs[6], offs[7], offs[8])], axis=0)
        y_sg = _spatial_gating(proj, sgu_ln_g[l], sgu_ln_b[l], sgu_w[l], sgu_b[l], bw, offs[9], offs[10])
        ys = jnp.stack([jnp.concatenate([y_na_p, y_na_s], axis=0), y_sc, y_sg,
                        jnp.concatenate([y_ga_p, y_ga_s], axis=0)], axis=0)

        merged = _gatebranch(h, w_gate_bf, b_gate[l], ys, w_branch_bf, l, tm_m, tn)
        x = _outproj(merged, w_out_bf, l, x_parts, mod, rows_m, tm_m, tn)

        h2, top_idx, gates, rank, counts = _router(x, mod, rows_n, tm_n, w_router[l], b_router[l])
        counts = counts[0].astype(jnp.int32)
        n_blk_e = (counts + MOE_SUPER - 1) // MOE_SUPER
        blk_start = jnp.cumsum(n_blk_e) - n_blk_e
        experts = jnp.arange(n_exp, dtype=jnp.int32)
        of_e = top_idx[:, :TOP_K, None] == experts
        dest = (jnp.sum(jnp.where(of_e, blk_start, 0), axis=-1) * MOE_SUPER + rank[:, :TOP_K]).reshape(-1)
        slot_tok = jnp.zeros((n_slots,), jnp.int32).at[dest].set(jnp.arange(n_assign, dtype=jnp.int32) // TOP_K)
        gblk = jnp.arange(n_slots // MOE_BLOCK, dtype=jnp.int32) * MOE_BLOCK
        in_e = (gblk[:, None] >= blk_start * MOE_SUPER) & (gblk[:, None] < (blk_start + n_blk_e) * MOE_SUPER)
        block_used = jnp.any(in_e & (gblk[:, None] - blk_start * MOE_SUPER < counts), axis=1).astype(jnp.int32)
        xs = _gather_rows(slot_tok, block_used, h2.reshape(t, d // LANES, LANES), n_slots)
        hs = _expert_up(_expert_schedule(counts, n_blk_e, blk_start, ff // tf, n_blocks), xs, w_gu, b_gu4, l, tf)
        yb = _expert_down(_expert_schedule(counts, n_blk_e, blk_start, d // tn_d, n_blocks), hs, w_down, b_down4,
                          l, tn_d)
        x = _combine(dest, rows_c, yb, gates, x, mod, tm_c)
        x_parts = (x,)

    y_prompt = _final_norm(x, final_norm, tm_n, 0, tp).reshape(bp, lp, d)
    y_sample = _final_norm(x, final_norm, tm_n, tp, ts).reshape(bs, ls, d)
    return (y_prompt, y_sample, jnp.stack(st_na_k, axis=1), jnp.stack(st_na_v, axis=1),
            jnp.stack(st_ga_k, axis=1), jnp.stack(st_ga_v, axis=1))
```

```python
import functools

import numpy as np
import jax
import jax.numpy as jnp
from jax import lax
from jax.experimental import pallas as pl
from jax.experimental.pallas import tpu as pltpu

F32 = jnp.float32
BF16 = jnp.bfloat16

LANES = 128
HEAD_DIM = 128
GRID_W = 64
NA_WIN_R = 8
NA_WIN_C = 16
N_KV_HEADS = 2
ROPE_THETA = 10000.0
CHUNK = 128
TOP_K = 4
SWIGLU_ALPHA = 1.702
SWIGLU_LIMIT = 7.0
MOE_BLOCK = 256
MOE_SUPER = 2 * MOE_BLOCK
NORM_EPS = 1e-6
NEG_INF = -1e30
ATT_SCALE = HEAD_DIM ** -0.5
NA_QROWS = 4
NA_KROWS = 12
VMEM_LIMIT = 56 * 1024 * 1024
SUBLANES = 8
ITEM_FILL, ITEM_COMPUTE, ITEM_NEW_WEIGHTS = 0, 1, 2
ISSUE_UNROLL = 8


def _cparams(sem):
    return pltpu.CompilerParams(dimension_semantics=sem, vmem_limit_bytes=VMEM_LIMIT)


def _pick(n, prefs):
    for p in prefs:
        if n % p == 0:
            return p
    return n


def _tile_mod_rows(tm, n_prompt, n_total, seq_sample):
    starts = np.arange(0, n_total, tm)
    return jnp.asarray(np.where(starts < n_prompt, 0, 1 + (starts - n_prompt) // seq_sample), jnp.int32)


def _rms(x):
    return x * lax.rsqrt(jnp.mean(x * x, axis=-1, keepdims=True) + NORM_EPS)


def _store_token_major(ref, val, first_row=0):
    rows, width = val.shape
    sub = width // LANES
    for s in range(sub):
        ref[pl.ds(first_row * sub + s, rows, stride=sub), :] = val[:, s * LANES:(s + 1) * LANES]


def _pack_bf16_halves(x):
    half = x.shape[1] // 2

    def bf16_bits(v):
        u = lax.bitcast_convert_type(v, jnp.uint32)
        return (u + (jnp.uint32(0x7FFF) + ((u >> 16) & jnp.uint32(1)))) >> 16

    return bf16_bits(x[:, :half]) | (bf16_bits(x[:, half:]) << 16)


def _unpack_bf16_halves(word):
    return (lax.bitcast_convert_type(word << 16, F32),
            lax.bitcast_convert_type(word & jnp.uint32(0xFFFF0000), F32))


def _store_token_major_3d(ref, val, first_row):
    _store_token_major(ref.reshape(ref.shape[0] * ref.shape[1], LANES), val, first_row)


def _row_pitch(sub):
    return sub + SUBLANES if sub % (2 * SUBLANES) == 0 else sub


def _load_lane_chunk(buf, first_row, rows, pitch, s):
    return buf[pl.ds(first_row * pitch + s, rows, stride=pitch), :]


def _ada_kernel(cv_ref, w_ref, b_ref, o_ref):
    cv = cv_ref[...]
    a = (cv * jax.nn.sigmoid(cv)).astype(BF16)
    o_ref[0] = jnp.dot(a, w_ref[0].astype(BF16), preferred_element_type=F32) + b_ref[0]


def _ada_table(cv, w_ada, b_ada):
    depth, d, n = w_ada.shape
    tn = _pick(n, (512, 256, 128))
    return pl.pallas_call(
        _ada_kernel,
        grid=(depth, n // tn),
        in_specs=[pl.BlockSpec((8, d), lambda l, j: (0, 0)),
                  pl.BlockSpec((1, d, tn), lambda l, j: (l, 0, j)),
                  pl.BlockSpec((1, 1, tn), lambda l, j: (l, 0, j))],
        out_specs=pl.BlockSpec((1, 8, tn), lambda l, j: (l, 0, j)),
        out_shape=jax.ShapeDtypeStruct((depth, 8, n), F32),
        compiler_params=_cparams(("parallel", "parallel")),
        name="ada_table",
    )(cv, w_ada, b_ada.reshape(depth, 1, n))


def _part_ranges(parts, tm):
    edges = np.cumsum([0] + [p.shape[0] // tm for p in parts])
    return [(int(lo), int(hi)) for lo, hi in zip(edges[:-1], edges[1:])]


def _in_part(i, lo, hi):
    return jnp.logical_and(i >= lo, i < hi)


def _normmod_kernel(rows_ref, *refs, ranges):
    del rows_ref
    x_refs = refs[:len(ranges)]
    scale_ref, shift_ref, o_ref = refs[len(ranges):]
    i = pl.program_id(0)
    for (lo, hi), x_ref in zip(ranges, x_refs):
        @pl.when(_in_part(i, lo, hi))
        def _(x_ref=x_ref):
            o_ref[...] = (_rms(x_ref[...]) * (1.0 + scale_ref[0]) + shift_ref[0]).astype(o_ref.dtype)


def _normmod(parts, mod, rows, tm, which_scale, which_shift, out_dtype):
    d = parts[0].shape[1]
    ranges = _part_ranges(parts, tm)
    n_tiles = ranges[-1][1]
    x_specs = [pl.BlockSpec((tm, d), functools.partial(lambda i, r, lo, hi: (jnp.clip(i - lo, 0, hi - lo - 1), 0),
                                                       lo=lo, hi=hi)) for lo, hi in ranges]
    return pl.pallas_call(
        functools.partial(_normmod_kernel, ranges=ranges),
        grid_spec=pltpu.PrefetchScalarGridSpec(
            num_scalar_prefetch=1, grid=(n_tiles,),
            in_specs=x_specs + [pl.BlockSpec((1, 1, d), lambda i, r: (r[i] * 6 + which_scale, 0, 0)),
                                pl.BlockSpec((1, 1, d), lambda i, r: (r[i] * 6 + which_shift, 0, 0))],
            out_specs=pl.BlockSpec((tm, d), lambda i, r: (i, 0))),
        out_shape=jax.ShapeDtypeStruct((n_tiles * tm, d), out_dtype),
        compiler_params=_cparams(("parallel",)),
        name="normmod",
    )(rows, *parts, mod, mod)


def _mm_kernel(a_ref, w_ref, o_ref):
    o_ref[...] = jnp.dot(a_ref[...], w_ref[0], preferred_element_type=F32).astype(o_ref.dtype)


def _matmul(a, w, layer, out_dtype, tm, tn):
    m, k = a.shape
    n = w.shape[2]
    return pl.pallas_call(
        _mm_kernel,
        grid=(m // tm, n // tn),
        in_specs=[pl.BlockSpec((tm, k), lambda i, j: (i, 0)),
                  pl.BlockSpec((1, k, tn), lambda i, j: (layer, 0, j))],
        out_specs=pl.BlockSpec((tm, tn), lambda i, j: (i, j)),
        out_shape=jax.ShapeDtypeStruct((m, n), out_dtype),
        compiler_params=_cparams(("parallel", "parallel")),
        name="in_proj",
    )(a, w)


def _outproj_kernel(rows_ref, a_ref, w_ref, *refs, ranges):
    del rows_ref
    x_refs = refs[:len(ranges)]
    g_ref, o_ref = refs[len(ranges):]
    i = pl.program_id(0)
    update = g_ref[0] * jnp.dot(a_ref[...], w_ref[0], preferred_element_type=F32)
    for (lo, hi), x_ref in zip(ranges, x_refs):
        @pl.when(_in_part(i, lo, hi))
        def _(x_ref=x_ref):
            o_ref[...] = x_ref[...] + update


def _outproj(a, w, layer, x_parts, mod, rows, tm, tn):
    m, k = a.shape
    n = w.shape[2]
    ranges = _part_ranges(x_parts, tm)

    def x_map(i, j, r, lo, hi):
        return (jnp.clip(i - lo, 0, hi - lo - 1), jnp.where(_in_part(i, lo, hi), j, 0))

    x_specs = [pl.BlockSpec((tm, tn), functools.partial(x_map, lo=lo, hi=hi)) for lo, hi in ranges]
    return pl.pallas_call(
        functools.partial(_outproj_kernel, ranges=ranges),
        grid_spec=pltpu.PrefetchScalarGridSpec(
            num_scalar_prefetch=1, grid=(m // tm, n // tn),
            in_specs=[pl.BlockSpec((tm, k), lambda i, j, r: (i, 0)),
                      pl.BlockSpec((1, k, tn), lambda i, j, r: (layer, 0, j))] + x_specs
                     + [pl.BlockSpec((1, 1, tn), lambda i, j, r: (r[i] * 6 + 2, 0, j))],
            out_specs=pl.BlockSpec((tm, tn), lambda i, j, r: (i, j))),
        out_shape=jax.ShapeDtypeStruct((m, n), F32),
        compiler_params=_cparams(("parallel", "parallel")),
        name="out_proj",
    )(rows, a, w, *x_parts, mod)


def _gatebranch_kernel(h_ref, wg_ref, bg_ref, y_ref, wb_ref, o_ref, acc_ref):
    br = pl.program_id(2)
    g = jnp.dot(h_ref[...], wg_ref[0, 0], preferred_element_type=F32) + bg_ref[0]
    p = jnp.dot(y_ref[0], wb_ref[0, 0], preferred_element_type=F32)
    v = jax.nn.sigmoid(g) * p

    @pl.when(br == 0)
    def _():
        acc_ref[...] = v

    @pl.when(br > 0)
    def _():
        acc_ref[...] += v

    @pl.when(br == pl.num_programs(2) - 1)
    def _():
        o_ref[...] = acc_ref[...].astype(o_ref.dtype)


def _gatebranch(h, w_gate, b_gate, ys, w_branch, layer, tm, tn):
    t, d = h.shape
    _, nbr, bw, _ = w_branch.shape
    return pl.pallas_call(
        _gatebranch_kernel,
        grid=(t // tm, d // tn, nbr),
        in_specs=[pl.BlockSpec((tm, d), lambda i, j, b: (i, 0)),
                  pl.BlockSpec((1, 1, d, tn), lambda i, j, b: (layer, b, 0, j)),
                  pl.BlockSpec((1, 1, tn), lambda i, j, b: (b, 0, j)),
                  pl.BlockSpec((1, tm, bw), lambda i, j, b: (b, i, 0)),
                  pl.BlockSpec((1, 1, bw, tn), lambda i, j, b: (layer, b, 0, j))],
        out_specs=pl.BlockSpec((tm, tn), lambda i, j, b: (i, j)),
        out_shape=jax.ShapeDtypeStruct((t, d), BF16),
        scratch_shapes=[pltpu.VMEM((tm, tn), F32)],
        compiler_params=_cparams(("parallel", "parallel", "arbitrary")),
        name="gate_branch",
    )(h, w_gate, b_gate.reshape(nbr, 1, d), ys, w_branch)


def _softmax_parts(scores):
    m = scores[0].max(axis=-1, keepdims=True)
    for s in scores[1:]:
        m = jnp.maximum(m, s.max(axis=-1, keepdims=True))
    ps = [jnp.exp(s - m) for s in scores]
    l = ps[0].sum(axis=-1, keepdims=True)
    for p in ps[1:]:
        l = l + p.sum(axis=-1, keepdims=True)
    return ps, l


def _qk(q, k):
    return lax.dot_general(q, k, (((1,), (1,)), ((), ())), preferred_element_type=F32) * ATT_SCALE


def _store_head_state(ref, head, n_heads, val):
    ref[pl.ds(head, val.shape[0], stride=n_heads), :] = val


def _ctx_na_kernel(q_ref, k_ref, v_ref, o_ref, ks_ref, vs_ref, *, heads):
    k, v = k_ref[...], v_ref[...]
    _store_head_state(ks_ref, pl.program_id(1), heads, k)
    _store_head_state(vs_ref, pl.program_id(1), heads, v)
    s = _qk(q_ref[...].astype(BF16), k.astype(BF16))
    (p,), l = _softmax_parts([s])
    o_ref[...] = (jnp.dot(p.astype(BF16), v.astype(BF16), preferred_element_type=F32) / l).astype(o_ref.dtype)


def _ctx_na_attention(proj, nb, seq, heads, cq, ck, cv):
    state = jax.ShapeDtypeStruct((nb * seq * heads, HEAD_DIM), F32)
    state_spec = pl.BlockSpec((seq * heads, HEAD_DIM), lambda b, h: (b, 0))
    return pl.pallas_call(
        functools.partial(_ctx_na_kernel, heads=heads),
        grid=(nb, heads),
        in_specs=[pl.BlockSpec((seq, HEAD_DIM), lambda b, h: (b, cq + h)),
                  pl.BlockSpec((seq, HEAD_DIM), lambda b, h: (b, ck + h)),
                  pl.BlockSpec((seq, HEAD_DIM), lambda b, h: (b, cv + h))],
        out_specs=[pl.BlockSpec((seq, HEAD_DIM), lambda b, h: (b, h)), state_spec, state_spec],
        out_shape=[jax.ShapeDtypeStruct((nb * seq, heads * HEAD_DIM), BF16), state, state],
        compiler_params=_cparams(("parallel", "arbitrary")),
        name="ctx_na_attn",
    )(proj, proj, proj)


def _ctx_gqa_kernel(q_ref, k_ref, v_ref, qn_ref, kn_ref, o_ref, ks_ref, vs_ref, *, group):
    kv = pl.program_id(1) // group
    v = v_ref[...]
    kn = _rms(k_ref[...]) * kn_ref[...]
    _store_head_state(ks_ref, kv, N_KV_HEADS, kn)
    _store_head_state(vs_ref, kv, N_KV_HEADS, v)
    qn = _rms(q_ref[...]) * qn_ref[...]
    s = _qk(qn.astype(BF16), kn.astype(BF16))
    (p,), l = _softmax_parts([s])
    o_ref[...] = (jnp.dot(p.astype(BF16), v.astype(BF16), preferred_element_type=F32) / l).astype(o_ref.dtype)


def _ctx_gqa_attention(proj, q_norm, k_norm, nb, seq, heads, group, cq, ck, cv):
    state = jax.ShapeDtypeStruct((nb * seq * N_KV_HEADS, HEAD_DIM), F32)
    state_spec = pl.BlockSpec((seq * N_KV_HEADS, HEAD_DIM), lambda b, h: (b, 0))
    return pl.pallas_call(
        functools.partial(_ctx_gqa_kernel, group=group),
        grid=(nb, heads),
        in_specs=[pl.BlockSpec((seq, HEAD_DIM), lambda b, h: (b, cq + h)),
                  pl.BlockSpec((seq, HEAD_DIM), lambda b, h: (b, ck + h // group)),
                  pl.BlockSpec((seq, HEAD_DIM), lambda b, h: (b, cv + h // group)),
                  pl.BlockSpec((1, HEAD_DIM), lambda b, h: (0, 0)),
                  pl.BlockSpec((1, HEAD_DIM), lambda b, h: (0, 0))],
        out_specs=[pl.BlockSpec((seq, HEAD_DIM), lambda b, h: (b, h)), state_spec, state_spec],
        out_shape=[jax.ShapeDtypeStruct((nb * seq, heads * HEAD_DIM), BF16), state, state],
        compiler_params=_cparams(("parallel", "arbitrary")),
        name="ctx_gqa_attn",
    )(proj, proj, proj, q_norm.reshape(1, HEAD_DIM), k_norm.reshape(1, HEAD_DIM))


def _rope(x, cos_t, sin_t):
    return x * cos_t + pltpu.roll(x, HEAD_DIM // 2, 1) * sin_t


def _lat_gqa_kernel(q_ref, k_ref, v_ref, kc_ref, vc_ref, cos_ref, sin_ref, qn_ref, kn_ref, o_ref, kr_ref, *, tq):
    qi = pl.program_id(2)

    @pl.when(qi == 0)
    def _():
        kn = _rms(k_ref[...]) * kn_ref[...]
        kr_ref[...] = _rope(kn, cos_ref[...], sin_ref[...]).astype(BF16)

    rows = pl.ds(pl.multiple_of(qi * tq, tq), tq)
    qn = _rms(q_ref[...]) * qn_ref[...]
    qr = _rope(qn, cos_ref[rows, :], sin_ref[rows, :])
    s_ctx = _qk(qn.astype(BF16), kc_ref[0, 0].astype(BF16))
    s_lat = _qk(qr.astype(BF16), kr_ref[...])
    (p_ctx, p_lat), l = _softmax_parts([s_ctx, s_lat])
    out = (jnp.dot(p_ctx.astype(BF16), vc_ref[0, 0].astype(BF16), preferred_element_type=F32)
           + jnp.dot(p_lat.astype(BF16), v_ref[...].astype(BF16), preferred_element_type=F32))
    o_ref[...] = (out / l).astype(o_ref.dtype)


def _lat_gqa_attention(proj, cache_k, cache_v, layer, cos_t, sin_t, q_norm, k_norm,
                       row0, nb, seq, heads, group, cq, ck, cv):
    tq = _pick(seq, (512, 256, 128))
    nq = seq // tq
    rb = row0 // tq
    sb = row0 // seq
    past = cache_k.shape[2]
    return pl.pallas_call(
        functools.partial(_lat_gqa_kernel, tq=tq),
        grid=(nb, heads, nq),
        in_specs=[pl.BlockSpec((tq, HEAD_DIM), lambda b, h, i: (rb + b * nq + i, cq + h)),
                  pl.BlockSpec((seq, HEAD_DIM), lambda b, h, i: (sb + b, ck + h // group)),
                  pl.BlockSpec((seq, HEAD_DIM), lambda b, h, i: (sb + b, cv + h // group)),
                  pl.BlockSpec((1, 1, past, HEAD_DIM), lambda b, h, i: (b, layer, 0, h // group)),
                  pl.BlockSpec((1, 1, past, HEAD_DIM), lambda b, h, i: (b, layer, 0, h // group)),
                  pl.BlockSpec((seq, HEAD_DIM), lambda b, h, i: (0, 0)),
                  pl.BlockSpec((seq, HEAD_DIM), lambda b, h, i: (0, 0)),
                  pl.BlockSpec((1, HEAD_DIM), lambda b, h, i: (0, 0)),
                  pl.BlockSpec((1, HEAD_DIM), lambda b, h, i: (0, 0))],
        out_specs=pl.BlockSpec((tq, HEAD_DIM), lambda b, h, i: (b * nq + i, h)),
        out_shape=jax.ShapeDtypeStruct((nb * seq, heads * HEAD_DIM), BF16),
        scratch_shapes=[pltpu.VMEM((seq, HEAD_DIM), BF16)],
        compiler_params=_cparams(("parallel", "arbitrary", "arbitrary")),
        name="lat_gqa_attn",
    )(proj, proj, proj, cache_k, cache_v, cos_t, sin_t, q_norm.reshape(1, HEAD_DIM), k_norm.reshape(1, HEAD_DIM))


def _na_key_row0(i, n_rows):
    return jnp.clip(NA_QROWS * i - NA_WIN_R // 2, 0, n_rows - NA_KROWS)


def _lat_na_kernel(q_ref, k_ref, v_ref, kc_ref, vc_ref, bias_ref, o_ref, *, n_rows):
    i = pl.program_id(2)
    start = pl.multiple_of(_na_key_row0(i, n_rows) * GRID_W, GRID_W)
    keys = pl.ds(start, NA_KROWS * GRID_W)
    q = q_ref[...].astype(BF16)
    s_ctx = _qk(q, kc_ref[0, 0].astype(BF16))
    s_win = _qk(q, k_ref[keys, :].astype(BF16)) + bias_ref[0, 0, 0]
    (p_ctx, p_win), l = _softmax_parts([s_ctx, s_win])
    out = (jnp.dot(p_ctx.astype(BF16), vc_ref[0, 0].astype(BF16), preferred_element_type=F32)
           + jnp.dot(p_win.astype(BF16), v_ref[keys, :].astype(BF16), preferred_element_type=F32))
    o_ref[...] = (out / l).astype(o_ref.dtype)


def _na_bias_tables(rpb, n_rows):
    n_layers, heads = rpb.shape[:2]
    n_blocks = n_rows // NA_QROWS
    pad = GRID_W - NA_WIN_C
    padded = jnp.pad(rpb.astype(F32), ((0, 0), (0, 0), (0, 0), (pad, pad)))
    ccol = jnp.stack([padded[..., GRID_W - 1 - qc:2 * GRID_W - 1 - qc] for qc in range(GRID_W)], axis=-2)
    qc = np.arange(GRID_W)[:, None]
    kc = np.arange(GRID_W)[None, :]
    cs = np.clip(qc - NA_WIN_C // 2, 0, GRID_W - NA_WIN_C)
    ccol = jnp.where((kc >= cs) & (kc < cs + NA_WIN_C), ccol, NEG_INF)
    masked = jnp.full((n_layers, heads, GRID_W, GRID_W), NEG_INF, F32)
    kinds = []
    for i in (0, min(2, n_blocks - 1), n_blocks - 1):
        k0 = int(np.clip(NA_QROWS * i - NA_WIN_R // 2, 0, n_rows - NA_KROWS))
        q_rows = []
        for qr in range(NA_QROWS * i, NA_QROWS * (i + 1)):
            rs = int(np.clip(qr - NA_WIN_R // 2, 0, n_rows - NA_WIN_R))
            q_rows.append(jnp.concatenate(
                [ccol[:, :, kr - qr + NA_WIN_R - 1] if rs <= kr < rs + NA_WIN_R else masked
                 for kr in range(k0, k0 + NA_KROWS)], axis=-1))
        kinds.append(jnp.concatenate(q_rows, axis=-2))
    return jnp.stack(kinds, axis=2)


def _lat_na_attention(proj, cache_k, cache_v, layer, bias, row0, nb, seq, heads, cq, ck, cv):
    n_rows = seq // GRID_W
    tq = NA_QROWS * GRID_W
    nq = seq // tq
    rb = row0 // tq
    sb = row0 // seq
    past = cache_k.shape[2]
    kw = NA_KROWS * GRID_W

    def bias_map(b, h, i):
        return (layer, h, jnp.where(i == 0, 0, jnp.where(i == nq - 1, 2, 1)), 0, 0)

    return pl.pallas_call(
        functools.partial(_lat_na_kernel, n_rows=n_rows),
        grid=(nb, heads, nq),
        in_specs=[pl.BlockSpec((tq, HEAD_DIM), lambda b, h, i: (rb + b * nq + i, cq + h)),
                  pl.BlockSpec((seq, HEAD_DIM), lambda b, h, i: (sb + b, ck + h)),
                  pl.BlockSpec((seq, HEAD_DIM), lambda b, h, i: (sb + b, cv + h)),
                  pl.BlockSpec((1, 1, past, HEAD_DIM), lambda b, h, i: (b, layer, 0, h)),
                  pl.BlockSpec((1, 1, past, HEAD_DIM), lambda b, h, i: (b, layer, 0, h)),
                  pl.BlockSpec((1, 1, 1, tq, kw), bias_map)],
        out_specs=pl.BlockSpec((tq, HEAD_DIM), lambda b, h, i: (b * nq + i, h)),
        out_shape=jax.ShapeDtypeStruct((nb * seq, heads * HEAD_DIM), BF16),
        compiler_params=_cparams(("parallel", "parallel", "arbitrary")),
        name="lat_na_attn",
    )(proj, proj, proj, cache_k, cache_v, bias)


def _conv_kernel(b_ref, c_ref, x_ref, w_ref, o_ref):
    p = c_ref[...] * x_ref[...]
    n = p.shape[0]
    row = lax.broadcasted_iota(jnp.int32, p.shape, 0)
    prev = jnp.where(row == 0, 0.0, pltpu.roll(p, 1, 0))
    nxt = jnp.where(row == n - 1, 0.0, pltpu.roll(p, n - 1, 0))
    w = w_ref[...]
    o_ref[...] = (b_ref[...] * (prev * w[0:1] + p * w[1:2] + nxt * w[2:3])).astype(o_ref.dtype)


def _short_conv(proj, conv_w, row0, nb, seq, width, off_b, off_c, off_x):
    tc = _pick(width, (256, 128))
    assert off_b % tc == 0 and off_c % tc == 0 and off_x % tc == 0
    sb = row0 // seq
    k = width // tc
    cb, cc, cx = off_b // tc, off_c // tc, off_x // tc
    return pl.pallas_call(
        _conv_kernel,
        grid=(nb, k),
        in_specs=[pl.BlockSpec((seq, tc), lambda b, j: (sb + b, cb + j)),
                  pl.BlockSpec((seq, tc), lambda b, j: (sb + b, cc + j)),
                  pl.BlockSpec((seq, tc), lambda b, j: (sb + b, cx + j)),
                  pl.BlockSpec((3, tc), lambda b, j: (0, j))],
        out_specs=pl.BlockSpec((seq, tc), lambda b, j: (b, j)),
        out_shape=jax.ShapeDtypeStruct((nb * seq, width), BF16),
        compiler_params=_cparams(("parallel", "parallel")),
        name="short_conv",
    )(proj, proj, proj, conv_w)


def _sgu_kernel(*refs, pieces):
    u_refs, v_refs = refs[:pieces], refs[pieces:2 * pieces]
    g_ref, b_ref, w_ref, bt_ref, o_ref = refs[2 * pieces:]
    rows, width = o_ref.shape
    groups = w_ref.shape[0]
    gw = width // groups
    pw = width // pieces
    for c in range(rows // CHUNK):
        rs = slice(c * CHUNK, (c + 1) * CHUNK)
        v = jnp.concatenate([r[rs, :] for r in v_refs], axis=-1)
        mu = jnp.mean(v, axis=-1, keepdims=True)
        var = jnp.mean(jnp.square(v - mu), axis=-1, keepdims=True)
        vn = ((v - mu) * lax.rsqrt(var + NORM_EPS)) * g_ref[...] + b_ref[...]
        for g in range(groups):
            cs = slice(g * gw, (g + 1) * gw)
            mixed = jnp.dot(w_ref[g].astype(BF16), vn[:, cs].astype(BF16), preferred_element_type=F32)
            mixed = mixed + bt_ref[:, g:g + 1]
            piece, lo = divmod(g * gw, pw)
            o_ref[rs, cs] = (u_refs[piece][rs, lo:lo + gw] * mixed).astype(o_ref.dtype)


def _spatial_gating(proj, ln_g, ln_b, w_s, b_s, width, off_u, off_v):
    t = proj.shape[0]
    rows = _pick(t, (256, 128))
    groups = w_s.shape[0]
    pw = int(np.gcd.reduce([width, off_u, off_v]))
    pieces = width // pw
    assert pw % (width // groups) == 0
    col_specs = [pl.BlockSpec((rows, pw), functools.partial(lambda i, c: (i, c), c=off // pw + p))
                 for off in (off_u, off_v) for p in range(pieces)]
    return pl.pallas_call(
        functools.partial(_sgu_kernel, pieces=pieces),
        grid=(t // rows,),
        in_specs=col_specs + [pl.BlockSpec((1, width), lambda i: (0, 0)),
                              pl.BlockSpec((1, width), lambda i: (0, 0)),
                              pl.BlockSpec((groups, CHUNK, CHUNK), lambda i: (0, 0, 0)),
                              pl.BlockSpec((CHUNK, groups), lambda i: (0, 0))],
        out_specs=pl.BlockSpec((rows, width), lambda i: (i, 0)),
        out_shape=jax.ShapeDtypeStruct((t, width), BF16),
        compiler_params=_cparams(("parallel",)),
        name="spatial_gating",
    )(*([proj] * (2 * pieces)), ln_g.reshape(1, width), ln_b.reshape(1, width), w_s, b_s.T)


def _router_kernel(rows_ref, x_ref, scale_ref, shift_ref, wr_ref, br_ref,
                   h_ref, idx_ref, gate_ref, rank_ref, cnt_ref, carry_ref):
    del rows_ref
    i = pl.program_id(0)

    @pl.when(i == 0)
    def _():
        carry_ref[...] = jnp.zeros_like(carry_ref)

    h = _rms(x_ref[...]) * (1.0 + scale_ref[0]) + shift_ref[0]
    _store_token_major(h_ref, _pack_bf16_halves(h))
    tm = h.shape[0]
    n_exp = wr_ref.shape[1]
    w = wr_ref[...]
    h_hi = h.astype(BF16)
    h_lo = (h - h_hi.astype(F32)).astype(BF16)
    w_hi = w.astype(BF16)
    w_lo = (w - w_hi.astype(F32)).astype(BF16)
    both = jnp.dot(h_hi, jnp.concatenate([w_hi, w_lo], axis=1), preferred_element_type=F32)
    logits = (both[:, :n_exp] + both[:, n_exp:]
              + jnp.dot(h_lo, w_hi, preferred_element_type=F32) + br_ref[...])
    lane_e = lax.broadcasted_iota(jnp.int32, (tm, n_exp), 1)
    lane_o = lax.broadcasted_iota(jnp.int32, (tm, LANES), 1)

    work = logits
    vals, sels = [], []
    for _ in range(TOP_K):
        m = work.max(axis=-1, keepdims=True)
        sel = jnp.min(jnp.where(work == m, lane_e, n_exp), axis=-1, keepdims=True)
        vals.append(m)
        sels.append(sel)
        work = jnp.where(lane_e == sel, -jnp.inf, work)
    exps = [jnp.exp(v - vals[0]) for v in vals]
    denom = exps[0]
    for e in exps[1:]:
        denom = denom + e

    onehot = jnp.zeros((tm, n_exp), F32)
    for sel in sels:
        onehot = onehot + (lane_e == sel).astype(F32)
    r = lax.broadcasted_iota(jnp.int32, (tm, tm), 0)
    c = lax.broadcasted_iota(jnp.int32, (tm, tm), 1)
    before = (r > c).astype(BF16)
    prefix = jnp.dot(before, onehot.astype(BF16), preferred_element_type=F32) + carry_ref[...]
    carry = carry_ref[...] + onehot.sum(axis=0, keepdims=True)
    carry_ref[...] = carry
    cnt_ref[...] = carry

    idx_out = jnp.zeros((tm, LANES), jnp.int32)
    gate_out = jnp.zeros((tm, LANES), F32)
    rank_out = jnp.zeros((tm, LANES), jnp.int32)
    for k in range(TOP_K):
        rank = jnp.sum(jnp.where(lane_e == sels[k], prefix, 0.0), axis=-1, keepdims=True).astype(jnp.int32)
        idx_out = jnp.where(lane_o == k, sels[k], idx_out)
        gate_out = jnp.where(lane_o == k, exps[k] / denom, gate_out)
        rank_out = jnp.where(lane_o == k, rank, rank_out)
    idx_ref[...] = idx_out
    gate_ref[...] = gate_out
    rank_ref[...] = rank_out


def _router(x, mod, rows, tm, w_router, b_router):
    t, d = x.shape
    n_exp = w_router.shape[1]
    return pl.pallas_call(
        _router_kernel,
        grid_spec=pltpu.PrefetchScalarGridSpec(
            num_scalar_prefetch=1, grid=(t // tm,),
            in_specs=[pl.BlockSpec((tm, d), lambda i, r: (i, 0)),
                      pl.BlockSpec((1, 1, d), lambda i, r: (r[i] * 6 + 4, 0, 0)),
                      pl.BlockSpec((1, 1, d), lambda i, r: (r[i] * 6 + 3, 0, 0)),
                      pl.BlockSpec((d, n_exp), lambda i, r: (0, 0)),
                      pl.BlockSpec((1, n_exp), lambda i, r: (0, 0))],
            out_specs=[pl.BlockSpec((tm * d // (2 * LANES), LANES), lambda i, r: (i, 0)),
                       pl.BlockSpec((tm, LANES), lambda i, r: (i, 0)),
                       pl.BlockSpec((tm, LANES), lambda i, r: (i, 0)),
                       pl.BlockSpec((tm, LANES), lambda i, r: (i, 0)),
                       pl.BlockSpec((1, n_exp), lambda i, r: (0, 0))],
            scratch_shapes=[pltpu.VMEM((1, n_exp), F32)]),
        out_shape=[jax.ShapeDtypeStruct((t * d // (2 * LANES), LANES), jnp.uint32),
                   jax.ShapeDtypeStruct((t, LANES), jnp.int32),
                   jax.ShapeDtypeStruct((t, LANES), F32),
                   jax.ShapeDtypeStruct((t, LANES), jnp.int32),
                   jax.ShapeDtypeStruct((1, n_exp), F32)],
        compiler_params=_cparams(("arbitrary",)),
        name="router",
    )(rows, x, mod, mod, w_router, b_router.reshape(1, n_exp))


def _wait_rows(buf_ref, n_rows, sub, sem):
    done = buf_ref.at[pl.ds(0, n_rows * sub), :]
    pltpu.make_async_copy(done, done, sem).wait()


def _gather_kernel(tok_ref, used_ref, src_ref, o_ref, buf0, buf1, sems):
    i = pl.program_id(0)
    last = pl.num_programs(0) - 1
    n_rows = o_ref.shape[0]
    sub = src_ref.shape[1]
    pitch = _row_pitch(sub)
    bufs = (buf0, buf1)

    def issue(block, slot):
        def body(g, carry):
            for j in range(ISSUE_UNROLL):
                r = g * ISSUE_UNROLL + j
                rows = pl.ds(pl.multiple_of(r * pitch, SUBLANES), sub)
                copy = pltpu.make_async_copy(src_ref.at[tok_ref[block * n_rows + r]], bufs[slot].at[rows, :], sems.at[slot])
                copy.start(priority=j % 2)
            return carry
        lax.fori_loop(0, n_rows // ISSUE_UNROLL, body, 0)

    for slot in (0, 1):
        @pl.when(i % 2 == slot)
        def _(slot=slot):
            if slot == 0:
                @pl.when(jnp.logical_and(i == 0, used_ref[0] == 1))
                def _():
                    issue(0, 0)

            @pl.when(jnp.logical_and(i < last, used_ref[jnp.minimum(i + 1, last)] == 1))
            def _():
                issue(i + 1, 1 - slot)

            @pl.when(used_ref[i] == 1)
            def _():
                _wait_rows(bufs[slot], n_rows, sub, sems.at[slot])
                half = sub * LANES
                for s in range(sub):
                    lo, hi = _unpack_bf16_halves(_load_lane_chunk(bufs[slot], 0, n_rows, pitch, s))
                    o_ref[:, s * LANES:(s + 1) * LANES] = lo.astype(o_ref.dtype)
                    o_ref[:, half + s * LANES:half + (s + 1) * LANES] = hi.astype(o_ref.dtype)

    @pl.when(used_ref[i] == 0)
    def _():
        o_ref[...] = jnp.zeros_like(o_ref)


def _gather_rows(slot_tok, block_used, src3, n_slots):
    _, sub, _ = src3.shape
    d = 2 * sub * LANES
    buf = pltpu.VMEM((MOE_BLOCK * _row_pitch(sub), LANES), jnp.uint32)
    return pl.pallas_call(
        _gather_kernel,
        grid_spec=pltpu.PrefetchScalarGridSpec(
            num_scalar_prefetch=2, grid=(n_slots // MOE_BLOCK,),
            in_specs=[pl.BlockSpec(memory_space=pl.ANY)],
            out_specs=pl.BlockSpec((MOE_BLOCK, d), lambda i, t, u: (i, 0)),
            scratch_shapes=[buf, buf, pltpu.SemaphoreType.DMA((2,))]),
        out_shape=jax.ShapeDtypeStruct((n_slots, d), BF16),
        compiler_params=_cparams(("arbitrary",)),
        name="expert_gather",
    )(slot_tok, block_used, src3)


N_SCHED = 8


def _stream_weights(sched, tile_copies, stage, wbf_refs):
    e_ref, t_ref, flag_ref, ne_ref, nt_ref = sched[0], sched[1], sched[4], sched[6], sched[7]
    w = pl.program_id(0)

    @pl.when(flag_ref[w] == ITEM_NEW_WEIGHTS)
    def _():
        @pl.when(w == 0)
        def _():
            for c in tile_copies(e_ref[0], t_ref[0]):
                c.start()

        for c in tile_copies(e_ref[w], t_ref[w]):
            c.wait()
        for p, ref in enumerate(wbf_refs):
            ref[...] = stage[p].astype(BF16)

        @pl.when(ne_ref[w] >= 0)
        def _():
            for c in tile_copies(ne_ref[w], nt_ref[w]):
                c.start()


def _run_item(sched, in_ref, compute, write):
    w = pl.program_id(0)
    real = sched[4][w] != ITEM_FILL
    first_only = sched[5][w] == 1
    half = in_ref.shape[0] // 2

    @pl.when(jnp.logical_and(real, jnp.logical_not(first_only)))
    def _():
        y = compute(in_ref[...])
        write(0, y[:half])
        write(1, y[half:])

    @pl.when(jnp.logical_and(real, first_only))
    def _():
        write(0, compute(in_ref[:half, :]))
        write(1, None)

    @pl.when(jnp.logical_not(real))
    def _():
        write(0, None)
        write(1, None)


def _expert_up_kernel(*refs, layer, tf, nf):
    sched = refs[:N_SCHED]
    x_ref, w_hbm, bg_ref, bu_ref, o_ref, stage, wg_bf, wu_bf, sems = refs[N_SCHED:]
    half = o_ref.shape[0] // 2

    def tile_copies(e, t):
        cols_g = pl.ds(pl.multiple_of(t * tf, tf), tf)
        cols_u = pl.ds(pl.multiple_of((nf + t) * tf, tf), tf)
        return (pltpu.make_async_copy(w_hbm.at[layer, e, :, cols_g], stage.at[0], sems.at[0]),
                pltpu.make_async_copy(w_hbm.at[layer, e, :, cols_u], stage.at[1], sems.at[1]))

    _stream_weights(sched, tile_copies, stage, (wg_bf, wu_bf))

    def compute(x):
        g = jnp.dot(x, wg_bf[...], preferred_element_type=F32) + bg_ref[0, 0]
        u = jnp.dot(x, wu_bf[...], preferred_element_type=F32) + bu_ref[0, 0]
        g = jnp.minimum(g, SWIGLU_LIMIT)
        u = jnp.clip(u, -SWIGLU_LIMIT, SWIGLU_LIMIT)
        return (g * jax.nn.sigmoid(SWIGLU_ALPHA * g) * (u + 1.0)).astype(BF16)

    def write(which, val):
        rows = pl.ds(which * half, half)
        o_ref[rows, :] = jnp.zeros((half, o_ref.shape[1]), o_ref.dtype) if val is None else val

    _run_item(sched, x_ref, compute, write)


def _expert_up(sched, xs, w_gu, b_gu4, layer, tf):
    n_slots, d = xs.shape
    ff = w_gu.shape[3] // 2
    nf = ff // tf
    n_items = sched[0].shape[0]
    return pl.pallas_call(
        functools.partial(_expert_up_kernel, layer=layer, tf=tf, nf=nf),
        grid_spec=pltpu.PrefetchScalarGridSpec(
            num_scalar_prefetch=N_SCHED, grid=(n_items,),
            in_specs=[pl.BlockSpec((MOE_SUPER, d), lambda w, e, t, b, *_: (b[w], 0)),
                      pl.BlockSpec(memory_space=pl.ANY),
                      pl.BlockSpec((1, 1, 1, tf), lambda w, e, t, *_: (layer, e[w], 0, t[w])),
                      pl.BlockSpec((1, 1, 1, tf), lambda w, e, t, *_: (layer, e[w], 0, nf + t[w]))],
            out_specs=pl.BlockSpec((MOE_SUPER, tf), lambda w, e, t, b, o, *_: (b[w], o[w])),
            scratch_shapes=[pltpu.VMEM((2, d, tf), F32), pltpu.VMEM((d, tf), BF16), pltpu.VMEM((d, tf), BF16),
                            pltpu.SemaphoreType.DMA((2,))]),
        out_shape=jax.ShapeDtypeStruct((n_slots, ff), BF16),
        compiler_params=_cparams(("arbitrary",)),
        name="expert_up",
    )(*sched, xs, w_gu, b_gu4, b_gu4)


def _expert_down_kernel(*refs, layer, tn):
    sched = refs[:N_SCHED]
    h_ref, w_hbm, bd_ref, o_ref, stage, wd_bf, sems = refs[N_SCHED:]
    half = h_ref.shape[0] // 2

    def tile_copies(e, t):
        cols = pl.ds(pl.multiple_of(t * tn, tn), tn)
        return (pltpu.make_async_copy(w_hbm.at[layer, e, :, cols], stage.at[0], sems.at[0]),)

    _stream_weights(sched, tile_copies, stage, (wd_bf,))

    def compute(h):
        return jnp.dot(h, wd_bf[...], preferred_element_type=F32) + bd_ref[0, 0]

    def write(which, val):
        if val is None:
            o_ref[pl.ds(which * half, half), :, :] = jnp.zeros((half,) + o_ref.shape[1:], o_ref.dtype)
        else:
            _store_token_major_3d(o_ref, _pack_bf16_halves(val), which * half)

    _run_item(sched, h_ref, compute, write)


def _expert_down(sched, hs, w_down, b_down4, layer, tn):
    n_slots, ff = hs.shape
    d = w_down.shape[3]
    n_items = sched[0].shape[0]
    return pl.pallas_call(
        functools.partial(_expert_down_kernel, layer=layer, tn=tn),
        grid_spec=pltpu.PrefetchScalarGridSpec(
            num_scalar_prefetch=N_SCHED, grid=(n_items,),
            in_specs=[pl.BlockSpec((MOE_SUPER, ff), lambda w, e, t, b, *_: (b[w], 0)),
                      pl.BlockSpec(memory_space=pl.ANY),
                      pl.BlockSpec((1, 1, 1, tn), lambda w, e, t, *_: (layer, e[w], 0, t[w]))],
            out_specs=pl.BlockSpec((MOE_SUPER, tn // (2 * LANES), LANES), lambda w, e, t, b, o, *_: (b[w], o[w], 0)),
            scratch_shapes=[pltpu.VMEM((1, ff, tn), F32), pltpu.VMEM((ff, tn), BF16),
                            pltpu.SemaphoreType.DMA((1,))]),
        out_shape=jax.ShapeDtypeStruct((n_slots, d // (2 * LANES), LANES), jnp.uint32),
        compiler_params=_cparams(("arbitrary",)),
        name="expert_down",
    )(*sched, hs, w_down, b_down4)


def _expert_schedule(counts, n_blk_e, blk_start, n_tiles, n_blocks):
    i32 = jnp.int32
    n_exp = n_blk_e.shape[0]
    items = n_blk_e * n_tiles
    item_end = jnp.cumsum(items)
    n_used = jnp.sum(n_blk_e)
    total = n_used * n_tiles
    w = jnp.arange(n_tiles * n_blocks, dtype=i32)
    wc = jnp.clip(w, 0, jnp.maximum(total - 1, 0))
    e = jnp.minimum(jnp.sum(item_end[None, :] <= wc[:, None], axis=1), n_exp - 1).astype(i32)
    of_e = e[:, None] == jnp.arange(n_exp, dtype=i32)[None, :]

    def lookup(table):
        return jnp.sum(jnp.where(of_e, table[None, :], 0), axis=1)

    local = wc - lookup(item_end - items)
    nbe = jnp.maximum(lookup(n_blk_e), 1)
    tile = local // nbe
    blk_start = lookup(blk_start)
    valid = w < total
    spare = jnp.maximum(w - total, 0)
    n_unused = jnp.maximum(n_blocks - n_used, 1)
    blk = jnp.where(valid, blk_start + local % nbe, n_used + spare % n_unused)
    out_tile = jnp.where(valid, tile, spare // n_unused)
    flag = jnp.where(valid, jnp.where(local % nbe == 0, ITEM_NEW_WEIGHTS, ITEM_COMPUTE), ITEM_FILL)
    first_only = valid & (lookup(counts) - (local % nbe) * MOE_SUPER <= MOE_SUPER // 2)
    experts = jnp.arange(n_exp, dtype=i32)
    nonempty = n_blk_e > 0
    before = jnp.cumsum(nonempty) - nonempty
    later = nonempty[None, :] & (experts[None, :] > experts[:, None])
    next_nonempty = jnp.min(jnp.where(later, experts[None, :], n_exp), axis=1)
    group = lookup(before) * n_tiles + tile
    last_tile = tile + 1 >= n_tiles
    next_e = jnp.where(last_tile, lookup(next_nonempty), e)
    next_e = jnp.where(group + 1 < jnp.sum(nonempty) * n_tiles, next_e, -1)
    next_tile = jnp.where(last_tile, 0, tile + 1)
    return tuple(a.astype(i32) for a in (e, tile, blk, out_tile, flag, first_only, next_e, next_tile))


def _combine_kernel(dest_ref, rows_ref, y_ref, gate_ref, x_ref, g2_ref, o_ref, buf0, buf1, sems, *, tn):
    del rows_ref
    bufs = (buf0, buf1)
    i = pl.program_id(0)
    tm = x_ref.shape[0]
    sub = y_ref.shape[1]
    pitch = _row_pitch(sub)

    def issue(tile, slot):
        def body(g, carry):
            for u in range(ISSUE_UNROLL):
                p = g * ISSUE_UNROLL + u
                k = p // tm
                t = p - k * tm
                row = dest_ref[(tile * tm + t) * TOP_K + k]
                rows = pl.ds(pl.multiple_of(p * pitch, SUBLANES), sub)
                copy = pltpu.make_async_copy(y_ref.at[row], bufs[slot].at[rows, :], sems.at[slot])
                copy.start(priority=u % 2)
            return carry
        lax.fori_loop(0, TOP_K * tm // ISSUE_UNROLL, body, 0)

    for slot in (0, 1):
        @pl.when(i % 2 == slot)
        def _(slot=slot):
            if slot == 0:
                @pl.when(i == 0)
                def _():
                    issue(0, 0)

            @pl.when(i + 1 < pl.num_programs(0))
            def _():
                issue(i + 1, 1 - slot)

            gate = gate_ref[...]
            buf = bufs[slot]
            _wait_rows(buf, TOP_K * tm, sub, sems.at[slot])
            words_per_tile = tn // (2 * LANES)
            for s in range(sub):
                lo0 = (s // words_per_tile) * tn + (s % words_per_tile) * LANES
                acc_lo = acc_hi = None
                for k in range(TOP_K):
                    lo, hi = _unpack_bf16_halves(_load_lane_chunk(buf, k * tm, tm, pitch, s))
                    g = gate[:, k:k + 1]
                    acc_lo = g * lo if k == 0 else acc_lo + g * lo
                    acc_hi = g * hi if k == 0 else acc_hi + g * hi
                for c0, acc in ((lo0, acc_lo), (lo0 + tn // 2, acc_hi)):
                    cols = slice(c0, c0 + LANES)
                    o_ref[:, cols] = x_ref[:, cols] + g2_ref[0, :, cols] * acc


def _combine(dest_flat, rows, y3, gates, x, mod, tm, tn):
    t, d = x.shape
    sub = y3.shape[1]
    buf = pltpu.VMEM((TOP_K * tm * _row_pitch(sub), LANES), jnp.uint32)
    return pl.pallas_call(
        functools.partial(_combine_kernel, tn=tn),
        grid_spec=pltpu.PrefetchScalarGridSpec(
            num_scalar_prefetch=2, grid=(t // tm,),
            in_specs=[pl.BlockSpec(memory_space=pl.ANY),
                      pl.BlockSpec((tm, LANES), lambda i, dst, r: (i, 0)),
                      pl.BlockSpec((tm, d), lambda i, dst, r: (i, 0)),
                      pl.BlockSpec((1, 1, d), lambda i, dst, r: (r[i] * 6 + 5, 0, 0))],
            out_specs=pl.BlockSpec((tm, d), lambda i, dst, r: (i, 0)),
            scratch_shapes=[buf, buf, pltpu.SemaphoreType.DMA((2,))]),
        out_shape=jax.ShapeDtypeStruct((t, d), F32),
        compiler_params=_cparams(("arbitrary",)),
        name="expert_combine",
    )(dest_flat, rows, y3, gates, x, mod)


def _final_kernel(x_ref, w_ref, o_ref):
    o_ref[...] = _rms(x_ref[...]) * w_ref[...]


def _final_norm(x, w, tm, row0, n_rows):
    d = x.shape[1]
    first = row0 // tm
    return pl.pallas_call(
        _final_kernel,
        grid=(n_rows // tm,),
        in_specs=[pl.BlockSpec((tm, d), lambda i: (first + i, 0)), pl.BlockSpec((1, d), lambda i: (0, 0))],
        out_specs=pl.BlockSpec((tm, d), lambda i: (i, 0)),
        out_shape=jax.ShapeDtypeStruct((n_rows, d), F32),
        compiler_params=_cparams(("parallel",)),
        name="final_norm",
    )(x, w.reshape(1, d))


def _rope_tables(length):
    pairs = HEAD_DIM // 4
    t = np.arange(length)
    row = (t // GRID_W).astype(np.float32)
    col = (t % GRID_W).astype(np.float32)
    inv = jnp.asarray(ROPE_THETA, F32) ** (-jnp.arange(pairs, dtype=F32) / pairs)
    ang = jnp.concatenate([jnp.asarray(row)[:, None] * inv, jnp.asarray(col)[:, None] * inv], axis=-1)
    cos, sin = jnp.cos(ang), jnp.sin(ang)
    return jnp.concatenate([cos, cos], axis=-1), jnp.concatenate([-sin, sin], axis=-1)


def kernel(x_prompt, x_sample, c, cache_na_k, cache_na_v, cache_gqa_k, cache_gqa_v, c_ctx, w_ada, b_ada, w_in, q_norm, k_norm, na_rpb, conv_w, sgu_w, sgu_b, sgu_ln_g, sgu_ln_b, w_branch, w_gate, b_gate, w_out, w_router, b_router, w_gu, b_gu, w_down, b_down, final_norm):
    bp, lp, d = x_prompt.shape
    bs, ls, _ = x_sample.shape
    depth = w_ada.shape[0]
    past = cache_na_k.shape[2]
    tp, ts = bp * lp, bs * ls
    t = tp + ts
    bw = w_branch.shape[2]
    heads = bw // HEAD_DIM
    group = heads // N_KV_HEADS
    n_exp = w_router.shape[2]
    ff = w_down.shape[2]
    n_rows = ls // GRID_W
    assert bw % LANES == 0 and sgu_w.shape[1] * CHUNK == bw and sgu_w.shape[2] == CHUNK

    widths = (bw, bw, bw, heads * HEAD_DIM, N_KV_HEADS * HEAD_DIM, N_KV_HEADS * HEAD_DIM, bw, bw, bw, bw, bw)
    offs = np.concatenate([[0], np.cumsum(widths)])
    blk = [int(o) // LANES for o in offs]
    offs = [int(o) for o in offs]
    assert tp % ls == 0

    x_parts = (x_prompt.reshape(tp, d), x_sample.reshape(ts, d))
    cv = jnp.concatenate([c_ctx[None, :], c, jnp.zeros((8 - 1 - bs, d), F32)], axis=0)
    mod_all = _ada_table(cv, w_ada, b_ada)
    cos_t, sin_t = _rope_tables(ls)
    bias = _na_bias_tables(na_rpb, n_rows)
    cna_k = cache_na_k.reshape(bs, depth, past, heads * HEAD_DIM)
    cna_v = cache_na_v.reshape(bs, depth, past, heads * HEAD_DIM)
    cga_k = cache_gqa_k.reshape(bs, depth, past, N_KV_HEADS * HEAD_DIM)
    cga_v = cache_gqa_v.reshape(bs, depth, past, N_KV_HEADS * HEAD_DIM)

    tm_n = _pick(lp, (256, 128))
    tm_m = _pick(ls, (1024, 512, 256))
    tm_m = tm_m if tp % tm_m == 0 else tm_n
    rows_n = _tile_mod_rows(tm_n, tp, t, ls)
    rows_m = _tile_mod_rows(tm_m, tp, t, ls)
    tm_c = tm_n
    rows_c = _tile_mod_rows(tm_c, tp, t, ls)
    tn = _pick(d, (512, 256, 128))
    tn_in = _pick(w_in.shape[2], (512, 256, 128))
    tf = _pick(ff, (512, 256, 128))
    tn_d = _pick(d, (2048, 1024, 512, 256, 128))
    n_assign = t * TOP_K
    n_blocks = -(-n_assign // MOE_SUPER) + n_exp
    n_slots = n_blocks * MOE_SUPER
    b_gu4 = b_gu.reshape(depth, n_exp, 1, 2 * ff)
    b_down4 = b_down.reshape(depth, n_exp, 1, d)
    w_in_bf, w_gate_bf, w_branch_bf, w_out_bf = (w.astype(BF16) for w in (w_in, w_gate, w_branch, w_out))

    st_na_k, st_na_v, st_ga_k, st_ga_v = [], [], [], []
    for l in range(depth):
        mod = mod_all[l].reshape(8 * 6, 1, d)
        h = _normmod(x_parts, mod, rows_n, tm_n, 1, 0, BF16)
        proj = _matmul(h, w_in_bf, l, F32, tm_m, tn_in)

        y_na_p, na_k_p, na_v_p = _ctx_na_attention(proj, bp, lp, heads, blk[0], blk[1], blk[2])
        y_ga_p, ga_k_p, ga_v_p = _ctx_gqa_attention(proj, q_norm[l], k_norm[l], bp, lp, heads, group,
                                                    blk[3], blk[4], blk[5])
        st_na_k.append(na_k_p.reshape(bp, lp, heads, HEAD_DIM))
        st_na_v.append(na_v_p.reshape(bp, lp, heads, HEAD_DIM))
        st_ga_k.append(ga_k_p.reshape(bp, lp, N_KV_HEADS, HEAD_DIM))
        st_ga_v.append(ga_v_p.reshape(bp, lp, N_KV_HEADS, HEAD_DIM))

        y_na_s = _lat_na_attention(proj, cna_k, cna_v, l, bias, tp, bs, ls, heads, blk[0], blk[1], blk[2])
        y_ga_s = _lat_gqa_attention(proj, cga_k, cga_v, l, cos_t, sin_t, q_norm[l], k_norm[l],
                                    tp, bs, ls, heads, group, blk[3], blk[4], blk[5])

        y_sc = jnp.concatenate([
            _short_conv(proj, conv_w[l], 0, bp, lp, bw, offs[6], offs[7], offs[8]),
            _short_conv(proj, conv_w[l], tp, bs, ls, bw, offs[6], offs[7], offs[8])], axis=0)
        y_sg = _spatial_gating(proj, sgu_ln_g[l], sgu_ln_b[l], sgu_w[l], sgu_b[l], bw, offs[9], offs[10])
        ys = jnp.stack([jnp.concatenate([y_na_p, y_na_s], axis=0), y_sc, y_sg,
                        jnp.concatenate([y_ga_p, y_ga_s], axis=0)], axis=0)

        merged = _gatebranch(h, w_gate_bf, b_gate[l], ys, w_branch_bf, l, tm_m, tn)
        x = _outproj(merged, w_out_bf, l, x_parts, mod, rows_m, tm_m, tn)

        h2, top_idx, gates, rank, counts = _router(x, mod, rows_n, tm_n, w_router[l], b_router[l])
        counts = counts[0].astype(jnp.int32)
        n_blk_e = (counts + MOE_SUPER - 1) // MOE_SUPER
        blk_start = jnp.cumsum(n_blk_e) - n_blk_e
        experts = jnp.arange(n_exp, dtype=jnp.int32)
        of_e = top_idx[:, :TOP_K, None] == experts
        dest = (jnp.sum(jnp.where(of_e, blk_start, 0), axis=-1) * MOE_SUPER + rank[:, :TOP_K]).reshape(-1)
        slot_tok = jnp.zeros((n_slots,), jnp.int32).at[dest].set(jnp.arange(n_assign, dtype=jnp.int32) // TOP_K)
        gblk = jnp.arange(n_slots // MOE_BLOCK, dtype=jnp.int32) * MOE_BLOCK
        in_e = (gblk[:, None] >= blk_start * MOE_SUPER) & (gblk[:, None] < (blk_start + n_blk_e) * MOE_SUPER)
        block_used = jnp.any(in_e & (gblk[:, None] - blk_start * MOE_SUPER < counts), axis=1).astype(jnp.int32)
        xs = _gather_rows(slot_tok, block_used, h2.reshape(t, d // (2 * LANES), LANES), n_slots)
        hs = _expert_up(_expert_schedule(counts, n_blk_e, blk_start, ff // tf, n_blocks), xs, w_gu, b_gu4, l, tf)
        yb = _expert_down(_expert_schedule(counts, n_blk_e, blk_start, d // tn_d, n_blocks), hs, w_down, b_down4,
                          l, tn_d)
        x = _combine(dest, rows_c, yb, gates, x, mod, tm_c, tn_d)
        x_parts = (x,)

    y_prompt = _final_norm(x, final_norm, tm_n, 0, tp).reshape(bp, lp, d)
    y_sample = _final_norm(x, final_norm, tm_n, tp, ts).reshape(bs, ls, d)
    return (y_prompt, y_sample, jnp.stack(st_na_k, axis=1), jnp.stack(st_na_v, axis=1),
            jnp.stack(st_ga_k, axis=1), jnp.stack(st_ga_v, axis=1))
```

```python
import functools

import numpy as np
import jax
import jax.numpy as jnp
from jax import lax
from jax.experimental import pallas as pl
from jax.experimental.pallas import tpu as pltpu

F32 = jnp.float32
BF16 = jnp.bfloat16

LANES = 128
HEAD_DIM = 128
GRID_W = 64
NA_WIN_R = 8
NA_WIN_C = 16
N_KV_HEADS = 2
ROPE_THETA = 10000.0
CHUNK = 128
TOP_K = 4
SWIGLU_ALPHA = 1.702
SWIGLU_LIMIT = 7.0
MOE_BLOCK = 256
MOE_SUPER = 2 * MOE_BLOCK
NORM_EPS = 1e-6
NEG_INF = -1e30
ATT_SCALE = HEAD_DIM ** -0.5
NA_QROWS = 4
NA_KROWS = 12
VMEM_LIMIT = 56 * 1024 * 1024
SUBLANES = 8
ITEM_FILL, ITEM_COMPUTE, ITEM_NEW_WEIGHTS = 0, 1, 2
ISSUE_UNROLL = 8


def _cparams(sem):
    return pltpu.CompilerParams(dimension_semantics=sem, vmem_limit_bytes=VMEM_LIMIT)


def _pick(n, prefs):
    for p in prefs:
        if n % p == 0:
            return p
    return n


def _tile_mod_rows(tm, n_prompt, n_total, seq_sample):
    starts = np.arange(0, n_total, tm)
    return jnp.asarray(np.where(starts < n_prompt, 0, 1 + (starts - n_prompt) // seq_sample), jnp.int32)


def _rms(x):
    return x * lax.rsqrt(jnp.mean(x * x, axis=-1, keepdims=True) + NORM_EPS)


def _store_token_major(ref, val, first_row=0):
    rows, width = val.shape
    sub = width // LANES
    for s in range(sub):
        ref[pl.ds(first_row * sub + s, rows, stride=sub), :] = val[:, s * LANES:(s + 1) * LANES]


def _pack_bf16_halves(x):
    half = x.shape[1] // 2

    def bf16_bits(v):
        u = lax.bitcast_convert_type(v, jnp.uint32)
        return (u + (jnp.uint32(0x7FFF) + ((u >> 16) & jnp.uint32(1)))) >> 16

    return bf16_bits(x[:, :half]) | (bf16_bits(x[:, half:]) << 16)


def _unpack_bf16_halves(word):
    return (lax.bitcast_convert_type(word << 16, F32),
            lax.bitcast_convert_type(word & jnp.uint32(0xFFFF0000), F32))


def _store_token_major_3d(ref, val, first_row):
    _store_token_major(ref.reshape(ref.shape[0] * ref.shape[1], LANES), val, first_row)


def _row_pitch(sub):
    return sub + SUBLANES if sub % (2 * SUBLANES) == 0 else sub


def _load_lane_chunk(buf, first_row, rows, pitch, s):
    return buf[pl.ds(first_row * pitch + s, rows, stride=pitch), :]


def _ada_kernel(cv_ref, w_ref, b_ref, o_ref):
    cv = cv_ref[...]
    a = (cv * jax.nn.sigmoid(cv)).astype(BF16)
    o_ref[0] = jnp.dot(a, w_ref[0].astype(BF16), preferred_element_type=F32) + b_ref[0]


def _ada_table(cv, w_ada, b_ada):
    depth, d, n = w_ada.shape
    tn = _pick(n, (512, 256, 128))
    return pl.pallas_call(
        _ada_kernel,
        grid=(depth, n // tn),
        in_specs=[pl.BlockSpec((8, d), lambda l, j: (0, 0)),
                  pl.BlockSpec((1, d, tn), lambda l, j: (l, 0, j)),
                  pl.BlockSpec((1, 1, tn), lambda l, j: (l, 0, j))],
        out_specs=pl.BlockSpec((1, 8, tn), lambda l, j: (l, 0, j)),
        out_shape=jax.ShapeDtypeStruct((depth, 8, n), F32),
        compiler_params=_cparams(("parallel", "parallel")),
        name="ada_table",
    )(cv, w_ada, b_ada.reshape(depth, 1, n))


def _part_ranges(parts, tm):
    edges = np.cumsum([0] + [p.shape[0] // tm for p in parts])
    return [(int(lo), int(hi)) for lo, hi in zip(edges[:-1], edges[1:])]


def _in_part(i, lo, hi):
    return jnp.logical_and(i >= lo, i < hi)


def _normmod_kernel(rows_ref, *refs, ranges):
    del rows_ref
    x_refs = refs[:len(ranges)]
    scale_ref, shift_ref, o_ref = refs[len(ranges):]
    i = pl.program_id(0)
    for (lo, hi), x_ref in zip(ranges, x_refs):
        @pl.when(_in_part(i, lo, hi))
        def _(x_ref=x_ref):
            o_ref[...] = (_rms(x_ref[...]) * (1.0 + scale_ref[0]) + shift_ref[0]).astype(o_ref.dtype)


def _normmod(parts, mod, rows, tm, which_scale, which_shift, out_dtype):
    d = parts[0].shape[1]
    ranges = _part_ranges(parts, tm)
    n_tiles = ranges[-1][1]
    x_specs = [pl.BlockSpec((tm, d), functools.partial(lambda i, r, lo, hi: (jnp.clip(i - lo, 0, hi - lo - 1), 0),
                                                       lo=lo, hi=hi)) for lo, hi in ranges]
    return pl.pallas_call(
        functools.partial(_normmod_kernel, ranges=ranges),
        grid_spec=pltpu.PrefetchScalarGridSpec(
            num_scalar_prefetch=1, grid=(n_tiles,),
            in_specs=x_specs + [pl.BlockSpec((1, 1, d), lambda i, r: (r[i] * 6 + which_scale, 0, 0)),
                                pl.BlockSpec((1, 1, d), lambda i, r: (r[i] * 6 + which_shift, 0, 0))],
            out_specs=pl.BlockSpec((tm, d), lambda i, r: (i, 0))),
        out_shape=jax.ShapeDtypeStruct((n_tiles * tm, d), out_dtype),
        compiler_params=_cparams(("parallel",)),
        name="normmod",
    )(rows, *parts, mod, mod)


def _mm_kernel(a_ref, w_ref, o_ref):
    o_ref[...] = jnp.dot(a_ref[...], w_ref[0], preferred_element_type=F32).astype(o_ref.dtype)


def _matmul(a, w, layer, out_dtype, tm, tn):
    m, k = a.shape
    n = w.shape[2]
    return pl.pallas_call(
        _mm_kernel,
        grid=(m // tm, n // tn),
        in_specs=[pl.BlockSpec((tm, k), lambda i, j: (i, 0)),
                  pl.BlockSpec((1, k, tn), lambda i, j: (layer, 0, j))],
        out_specs=pl.BlockSpec((tm, tn), lambda i, j: (i, j)),
        out_shape=jax.ShapeDtypeStruct((m, n), out_dtype),
        compiler_params=_cparams(("parallel", "parallel")),
        name="in_proj",
    )(a, w)


def _outproj_kernel(rows_ref, a_ref, w_ref, *refs, ranges):
    del rows_ref
    x_refs = refs[:len(ranges)]
    g_ref, o_ref = refs[len(ranges):]
    i = pl.program_id(0)
    update = g_ref[0] * jnp.dot(a_ref[...], w_ref[0], preferred_element_type=F32)
    for (lo, hi), x_ref in zip(ranges, x_refs):
        @pl.when(_in_part(i, lo, hi))
        def _(x_ref=x_ref):
            o_ref[...] = x_ref[...] + update


def _outproj(a, w, layer, x_parts, mod, rows, tm, tn):
    m, k = a.shape
    n = w.shape[2]
    ranges = _part_ranges(x_parts, tm)

    def x_map(i, j, r, lo, hi):
        return (jnp.clip(i - lo, 0, hi - lo - 1), jnp.where(_in_part(i, lo, hi), j, 0))

    x_specs = [pl.BlockSpec((tm, tn), functools.partial(x_map, lo=lo, hi=hi)) for lo, hi in ranges]
    return pl.pallas_call(
        functools.partial(_outproj_kernel, ranges=ranges),
        grid_spec=pltpu.PrefetchScalarGridSpec(
            num_scalar_prefetch=1, grid=(m // tm, n // tn),
            in_specs=[pl.BlockSpec((tm, k), lambda i, j, r: (i, 0)),
                      pl.BlockSpec((1, k, tn), lambda i, j, r: (layer, 0, j))] + x_specs
                     + [pl.BlockSpec((1, 1, tn), lambda i, j, r: (r[i] * 6 + 2, 0, j))],
            out_specs=pl.BlockSpec((tm, tn), lambda i, j, r: (i, j))),
        out_shape=jax.ShapeDtypeStruct((m, n), F32),
        compiler_params=_cparams(("parallel", "parallel")),
        name="out_proj",
    )(rows, a, w, *x_parts, mod)


def _gatebranch_kernel(h_ref, wg_ref, bg_ref, y_ref, wb_ref, o_ref, acc_ref):
    br = pl.program_id(2)
    g = jnp.dot(h_ref[...], wg_ref[0, 0], preferred_element_type=F32) + bg_ref[0]
    p = jnp.dot(y_ref[0], wb_ref[0, 0], preferred_element_type=F32)
    v = jax.nn.sigmoid(g) * p

    @pl.when(br == 0)
    def _():
        acc_ref[...] = v

    @pl.when(br > 0)
    def _():
        acc_ref[...] += v

    @pl.when(br == pl.num_programs(2) - 1)
    def _():
        o_ref[...] = acc_ref[...].astype(o_ref.dtype)


def _gatebranch(h, w_gate, b_gate, ys, w_branch, layer, tm, tn):
    t, d = h.shape
    _, nbr, bw, _ = w_branch.shape
    return pl.pallas_call(
        _gatebranch_kernel,
        grid=(t // tm, d // tn, nbr),
        in_specs=[pl.BlockSpec((tm, d), lambda i, j, b: (i, 0)),
                  pl.BlockSpec((1, 1, d, tn), lambda i, j, b: (layer, b, 0, j)),
                  pl.BlockSpec((1, 1, tn), lambda i, j, b: (b, 0, j)),
                  pl.BlockSpec((1, tm, bw), lambda i, j, b: (b, i, 0)),
                  pl.BlockSpec((1, 1, bw, tn), lambda i, j, b: (layer, b, 0, j))],
        out_specs=pl.BlockSpec((tm, tn), lambda i, j, b: (i, j)),
        out_shape=jax.ShapeDtypeStruct((t, d), BF16),
        scratch_shapes=[pltpu.VMEM((tm, tn), F32)],
        compiler_params=_cparams(("parallel", "parallel", "arbitrary")),
        name="gate_branch",
    )(h, w_gate, b_gate.reshape(nbr, 1, d), ys, w_branch)


def _softmax_parts(scores):
    m = scores[0].max(axis=-1, keepdims=True)
    for s in scores[1:]:
        m = jnp.maximum(m, s.max(axis=-1, keepdims=True))
    ps = [jnp.exp(s - m) for s in scores]
    l = ps[0].sum(axis=-1, keepdims=True)
    for p in ps[1:]:
        l = l + p.sum(axis=-1, keepdims=True)
    return ps, l


def _qk(q, k):
    return lax.dot_general(q, k, (((1,), (1,)), ((), ())), preferred_element_type=F32) * ATT_SCALE


def _store_head_state(ref, head, n_heads, val):
    ref[pl.ds(head, val.shape[0], stride=n_heads), :] = val


def _ctx_na_kernel(q_ref, k_ref, v_ref, o_ref, ks_ref, vs_ref, *, heads):
    k, v = k_ref[...], v_ref[...]
    _store_head_state(ks_ref, pl.program_id(1), heads, k)
    _store_head_state(vs_ref, pl.program_id(1), heads, v)
    s = _qk(q_ref[...].astype(BF16), k.astype(BF16))
    (p,), l = _softmax_parts([s])
    o_ref[...] = (jnp.dot(p.astype(BF16), v.astype(BF16), preferred_element_type=F32) / l).astype(o_ref.dtype)


def _ctx_na_attention(proj, nb, seq, heads, cq, ck, cv):
    state = jax.ShapeDtypeStruct((nb * seq * heads, HEAD_DIM), F32)
    state_spec = pl.BlockSpec((seq * heads, HEAD_DIM), lambda b, h: (b, 0))
    return pl.pallas_call(
        functools.partial(_ctx_na_kernel, heads=heads),
        grid=(nb, heads),
        in_specs=[pl.BlockSpec((seq, HEAD_DIM), lambda b, h: (b, cq + h)),
                  pl.BlockSpec((seq, HEAD_DIM), lambda b, h: (b, ck + h)),
                  pl.BlockSpec((seq, HEAD_DIM), lambda b, h: (b, cv + h))],
        out_specs=[pl.BlockSpec((seq, HEAD_DIM), lambda b, h: (b, h)), state_spec, state_spec],
        out_shape=[jax.ShapeDtypeStruct((nb * seq, heads * HEAD_DIM), BF16), state, state],
        compiler_params=_cparams(("parallel", "arbitrary")),
        name="ctx_na_attn",
    )(proj, proj, proj)


def _ctx_gqa_kernel(q_ref, k_ref, v_ref, qn_ref, kn_ref, o_ref, ks_ref, vs_ref, *, group):
    kv = pl.program_id(1) // group
    v = v_ref[...]
    kn = _rms(k_ref[...]) * kn_ref[...]
    _store_head_state(ks_ref, kv, N_KV_HEADS, kn)
    _store_head_state(vs_ref, kv, N_KV_HEADS, v)
    qn = _rms(q_ref[...]) * qn_ref[...]
    s = _qk(qn.astype(BF16), kn.astype(BF16))
    (p,), l = _softmax_parts([s])
    o_ref[...] = (jnp.dot(p.astype(BF16), v.astype(BF16), preferred_element_type=F32) / l).astype(o_ref.dtype)


def _ctx_gqa_attention(proj, q_norm, k_norm, nb, seq, heads, group, cq, ck, cv):
    state = jax.ShapeDtypeStruct((nb * seq * N_KV_HEADS, HEAD_DIM), F32)
    state_spec = pl.BlockSpec((seq * N_KV_HEADS, HEAD_DIM), lambda b, h: (b, 0))
    return pl.pallas_call(
        functools.partial(_ctx_gqa_kernel, group=group),
        grid=(nb, heads),
        in_specs=[pl.BlockSpec((seq, HEAD_DIM), lambda b, h: (b, cq + h)),
                  pl.BlockSpec((seq, HEAD_DIM), lambda b, h: (b, ck + h // group)),
                  pl.BlockSpec((seq, HEAD_DIM), lambda b, h: (b, cv + h // group)),
                  pl.BlockSpec((1, HEAD_DIM), lambda b, h: (0, 0)),
                  pl.BlockSpec((1, HEAD_DIM), lambda b, h: (0, 0))],
        out_specs=[pl.BlockSpec((seq, HEAD_DIM), lambda b, h: (b, h)), state_spec, state_spec],
        out_shape=[jax.ShapeDtypeStruct((nb * seq, heads * HEAD_DIM), BF16), state, state],
        compiler_params=_cparams(("parallel", "arbitrary")),
        name="ctx_gqa_attn",
    )(proj, proj, proj, q_norm.reshape(1, HEAD_DIM), k_norm.reshape(1, HEAD_DIM))


def _rope(x, cos_t, sin_t):
    return x * cos_t + pltpu.roll(x, HEAD_DIM // 2, 1) * sin_t


def _lat_gqa_kernel(q_ref, k_ref, v_ref, kc_ref, vc_ref, cos_ref, sin_ref, qn_ref, kn_ref, o_ref, kr_ref, *, tq):
    qi = pl.program_id(2)

    @pl.when(qi == 0)
    def _():
        kn = _rms(k_ref[...]) * kn_ref[...]
        kr_ref[...] = _rope(kn, cos_ref[...], sin_ref[...]).astype(BF16)

    rows = pl.ds(pl.multiple_of(qi * tq, tq), tq)
    qn = _rms(q_ref[...]) * qn_ref[...]
    qr = _rope(qn, cos_ref[rows, :], sin_ref[rows, :])
    s_ctx = _qk(qn.astype(BF16), kc_ref[0, 0].astype(BF16))
    s_lat = _qk(qr.astype(BF16), kr_ref[...])
    (p_ctx, p_lat), l = _softmax_parts([s_ctx, s_lat])
    out = (jnp.dot(p_ctx.astype(BF16), vc_ref[0, 0].astype(BF16), preferred_element_type=F32)
           + jnp.dot(p_lat.astype(BF16), v_ref[...].astype(BF16), preferred_element_type=F32))
    o_ref[...] = (out / l).astype(o_ref.dtype)


def _lat_gqa_attention(proj, cache_k, cache_v, layer, cos_t, sin_t, q_norm, k_norm,
                       row0, nb, seq, heads, group, cq, ck, cv):
    tq = _pick(seq, (512, 256, 128))
    nq = seq // tq
    rb = row0 // tq
    sb = row0 // seq
    past = cache_k.shape[2]
    return pl.pallas_call(
        functools.partial(_lat_gqa_kernel, tq=tq),
        grid=(nb, heads, nq),
        in_specs=[pl.BlockSpec((tq, HEAD_DIM), lambda b, h, i: (rb + b * nq + i, cq + h)),
                  pl.BlockSpec((seq, HEAD_DIM), lambda b, h, i: (sb + b, ck + h // group)),
                  pl.BlockSpec((seq, HEAD_DIM), lambda b, h, i: (sb + b, cv + h // group)),
                  pl.BlockSpec((1, 1, past, HEAD_DIM), lambda b, h, i: (b, layer, 0, h // group)),
                  pl.BlockSpec((1, 1, past, HEAD_DIM), lambda b, h, i: (b, layer, 0, h // group)),
                  pl.BlockSpec((seq, HEAD_DIM), lambda b, h, i: (0, 0)),
                  pl.BlockSpec((seq, HEAD_DIM), lambda b, h, i: (0, 0)),
                  pl.BlockSpec((1, HEAD_DIM), lambda b, h, i: (0, 0)),
                  pl.BlockSpec((1, HEAD_DIM), lambda b, h, i: (0, 0))],
        out_specs=pl.BlockSpec((tq, HEAD_DIM), lambda b, h, i: (b * nq + i, h)),
        out_shape=jax.ShapeDtypeStruct((nb * seq, heads * HEAD_DIM), BF16),
        scratch_shapes=[pltpu.VMEM((seq, HEAD_DIM), BF16)],
        compiler_params=_cparams(("parallel", "arbitrary", "arbitrary")),
        name="lat_gqa_attn",
    )(proj, proj, proj, cache_k, cache_v, cos_t, sin_t, q_norm.reshape(1, HEAD_DIM), k_norm.reshape(1, HEAD_DIM))


def _na_key_row0(i, n_rows):
    return jnp.clip(NA_QROWS * i - NA_WIN_R // 2, 0, n_rows - NA_KROWS)


def _lat_na_kernel(q_ref, k_ref, v_ref, kc_ref, vc_ref, bias_ref, o_ref, *, n_rows):
    i = pl.program_id(2)
    start = pl.multiple_of(_na_key_row0(i, n_rows) * GRID_W, GRID_W)
    keys = pl.ds(start, NA_KROWS * GRID_W)
    q = q_ref[...].astype(BF16)
    s_ctx = _qk(q, kc_ref[0, 0].astype(BF16))
    s_win = _qk(q, k_ref[keys, :].astype(BF16)) + bias_ref[0, 0, 0]
    (p_ctx, p_win), l = _softmax_parts([s_ctx, s_win])
    out = (jnp.dot(p_ctx.astype(BF16), vc_ref[0, 0].astype(BF16), preferred_element_type=F32)
           + jnp.dot(p_win.astype(BF16), v_ref[keys, :].astype(BF16), preferred_element_type=F32))
    o_ref[...] = (out / l).astype(o_ref.dtype)


def _na_bias_tables(rpb, n_rows):
    n_layers, heads = rpb.shape[:2]
    n_blocks = n_rows // NA_QROWS
    pad = GRID_W - NA_WIN_C
    padded = jnp.pad(rpb.astype(F32), ((0, 0), (0, 0), (0, 0), (pad, pad)))
    ccol = jnp.stack([padded[..., GRID_W - 1 - qc:2 * GRID_W - 1 - qc] for qc in range(GRID_W)], axis=-2)
    qc = np.arange(GRID_W)[:, None]
    kc = np.arange(GRID_W)[None, :]
    cs = np.clip(qc - NA_WIN_C // 2, 0, GRID_W - NA_WIN_C)
    ccol = jnp.where((kc >= cs) & (kc < cs + NA_WIN_C), ccol, NEG_INF)
    masked = jnp.full((n_layers, heads, GRID_W, GRID_W), NEG_INF, F32)
    kinds = []
    for i in (0, min(2, n_blocks - 1), n_blocks - 1):
        k0 = int(np.clip(NA_QROWS * i - NA_WIN_R // 2, 0, n_rows - NA_KROWS))
        q_rows = []
        for qr in range(NA_QROWS * i, NA_QROWS * (i + 1)):
            rs = int(np.clip(qr - NA_WIN_R // 2, 0, n_rows - NA_WIN_R))
            q_rows.append(jnp.concatenate(
                [ccol[:, :, kr - qr + NA_WIN_R - 1] if rs <= kr < rs + NA_WIN_R else masked
                 for kr in range(k0, k0 + NA_KROWS)], axis=-1))
        kinds.append(jnp.concatenate(q_rows, axis=-2))
    return jnp.stack(kinds, axis=2)


def _lat_na_attention(proj, cache_k, cache_v, layer, bias, row0, nb, seq, heads, cq, ck, cv):
    n_rows = seq // GRID_W
    tq = NA_QROWS * GRID_W
    nq = seq // tq
    rb = row0 // tq
    sb = row0 // seq
    past = cache_k.shape[2]
    kw = NA_KROWS * GRID_W

    def bias_map(b, h, i):
        return (layer, h, jnp.where(i == 0, 0, jnp.where(i == nq - 1, 2, 1)), 0, 0)

    return pl.pallas_call(
        functools.partial(_lat_na_kernel, n_rows=n_rows),
        grid=(nb, heads, nq),
        in_specs=[pl.BlockSpec((tq, HEAD_DIM), lambda b, h, i: (rb + b * nq + i, cq + h)),
                  pl.BlockSpec((seq, HEAD_DIM), lambda b, h, i: (sb + b, ck + h)),
                  pl.BlockSpec((seq, HEAD_DIM), lambda b, h, i: (sb + b, cv + h)),
                  pl.BlockSpec((1, 1, past, HEAD_DIM), lambda b, h, i: (b, layer, 0, h)),
                  pl.BlockSpec((1, 1, past, HEAD_DIM), lambda b, h, i: (b, layer, 0, h)),
                  pl.BlockSpec((1, 1, 1, tq, kw), bias_map)],
        out_specs=pl.BlockSpec((tq, HEAD_DIM), lambda b, h, i: (b * nq + i, h)),
        out_shape=jax.ShapeDtypeStruct((nb * seq, heads * HEAD_DIM), BF16),
        compiler_params=_cparams(("parallel", "parallel", "arbitrary")),
        name="lat_na_attn",
    )(proj, proj, proj, cache_k, cache_v, bias)


def _conv_kernel(b_ref, c_ref, x_ref, w_ref, o_ref):
    p = c_ref[...] * x_ref[...]
    n = p.shape[0]
    row = lax.broadcasted_iota(jnp.int32, p.shape, 0)
    prev = jnp.where(row == 0, 0.0, pltpu.roll(p, 1, 0))
    nxt = jnp.where(row == n - 1, 0.0, pltpu.roll(p, n - 1, 0))
    w = w_ref[...]
    o_ref[...] = (b_ref[...] * (prev * w[0:1] + p * w[1:2] + nxt * w[2:3])).astype(o_ref.dtype)


def _short_conv(proj, conv_w, row0, nb, seq, width, off_b, off_c, off_x):
    tc = _pick(width, (256, 128))
    assert off_b % tc == 0 and off_c % tc == 0 and off_x % tc == 0
    sb = row0 // seq
    k = width // tc
    cb, cc, cx = off_b // tc, off_c // tc, off_x // tc
    return pl.pallas_call(
        _conv_kernel,
        grid=(nb, k),
        in_specs=[pl.BlockSpec((seq, tc), lambda b, j: (sb + b, cb + j)),
                  pl.BlockSpec((seq, tc), lambda b, j: (sb + b, cc + j)),
                  pl.BlockSpec((seq, tc), lambda b, j: (sb + b, cx + j)),
                  pl.BlockSpec((3, tc), lambda b, j: (0, j))],
        out_specs=pl.BlockSpec((seq, tc), lambda b, j: (b, j)),
        out_shape=jax.ShapeDtypeStruct((nb * seq, width), BF16),
        compiler_params=_cparams(("parallel", "parallel")),
        name="short_conv",
    )(proj, proj, proj, conv_w)


def _sgu_kernel(*refs, pieces):
    u_refs, v_refs = refs[:pieces], refs[pieces:2 * pieces]
    g_ref, b_ref, w_ref, bt_ref, o_ref = refs[2 * pieces:]
    rows, width = o_ref.shape
    groups = w_ref.shape[0]
    gw = width // groups
    pw = width // pieces
    for c in range(rows // CHUNK):
        rs = slice(c * CHUNK, (c + 1) * CHUNK)
        v = jnp.concatenate([r[rs, :] for r in v_refs], axis=-1)
        mu = jnp.mean(v, axis=-1, keepdims=True)
        var = jnp.mean(jnp.square(v - mu), axis=-1, keepdims=True)
        vn = ((v - mu) * lax.rsqrt(var + NORM_EPS)) * g_ref[...] + b_ref[...]
        for g in range(groups):
            cs = slice(g * gw, (g + 1) * gw)
            mixed = jnp.dot(w_ref[g].astype(BF16), vn[:, cs].astype(BF16), preferred_element_type=F32)
            mixed = mixed + bt_ref[:, g:g + 1]
            piece, lo = divmod(g * gw, pw)
            o_ref[rs, cs] = (u_refs[piece][rs, lo:lo + gw] * mixed).astype(o_ref.dtype)


def _spatial_gating(proj, ln_g, ln_b, w_s, b_s, width, off_u, off_v):
    t = proj.shape[0]
    rows = _pick(t, (256, 128))
    groups = w_s.shape[0]
    pw = int(np.gcd.reduce([width, off_u, off_v]))
    pieces = width // pw
    assert pw % (width // groups) == 0
    col_specs = [pl.BlockSpec((rows, pw), functools.partial(lambda i, c: (i, c), c=off // pw + p))
                 for off in (off_u, off_v) for p in range(pieces)]
    return pl.pallas_call(
        functools.partial(_sgu_kernel, pieces=pieces),
        grid=(t // rows,),
        in_specs=col_specs + [pl.BlockSpec((1, width), lambda i: (0, 0)),
                              pl.BlockSpec((1, width), lambda i: (0, 0)),
                              pl.BlockSpec((groups, CHUNK, CHUNK), lambda i: (0, 0, 0)),
                              pl.BlockSpec((CHUNK, groups), lambda i: (0, 0))],
        out_specs=pl.BlockSpec((rows, width), lambda i: (i, 0)),
        out_shape=jax.ShapeDtypeStruct((t, width), BF16),
        compiler_params=_cparams(("parallel",)),
        name="spatial_gating",
    )(*([proj] * (2 * pieces)), ln_g.reshape(1, width), ln_b.reshape(1, width), w_s, b_s.T)


def _router_kernel(rows_ref, x_ref, scale_ref, shift_ref, wr_ref, br_ref,
                   h_ref, idx_ref, gate_ref, rank_ref, cnt_ref, carry_ref):
    del rows_ref
    i = pl.program_id(0)

    @pl.when(i == 0)
    def _():
        carry_ref[...] = jnp.zeros_like(carry_ref)

    h = _rms(x_ref[...]) * (1.0 + scale_ref[0]) + shift_ref[0]
    _store_token_major(h_ref, _pack_bf16_halves(h))
    tm = h.shape[0]
    n_exp = wr_ref.shape[1]
    w = wr_ref[...]
    h_hi = h.astype(BF16)
    h_lo = (h - h_hi.astype(F32)).astype(BF16)
    w_hi = w.astype(BF16)
    w_lo = (w - w_hi.astype(F32)).astype(BF16)
    both = jnp.dot(h_hi, jnp.concatenate([w_hi, w_lo], axis=1), preferred_element_type=F32)
    logits = (both[:, :n_exp] + both[:, n_exp:]
              + jnp.dot(h_lo, w_hi, preferred_element_type=F32) + br_ref[...])
    lane_e = lax.broadcasted_iota(jnp.int32, (tm, n_exp), 1)
    lane_o = lax.broadcasted_iota(jnp.int32, (tm, LANES), 1)

    work = logits
    vals, sels = [], []
    for _ in range(TOP_K):
        m = work.max(axis=-1, keepdims=True)
        sel = jnp.min(jnp.where(work == m, lane_e, n_exp), axis=-1, keepdims=True)
        vals.append(m)
        sels.append(sel)
        work = jnp.where(lane_e == sel, -jnp.inf, work)
    exps = [jnp.exp(v - vals[0]) for v in vals]
    denom = exps[0]
    for e in exps[1:]:
        denom = denom + e

    onehot = jnp.zeros((tm, n_exp), F32)
    for sel in sels:
        onehot = onehot + (lane_e == sel).astype(F32)
    r = lax.broadcasted_iota(jnp.int32, (tm, tm), 0)
    c = lax.broadcasted_iota(jnp.int32, (tm, tm), 1)
    before = (r > c).astype(BF16)
    prefix = jnp.dot(before, onehot.astype(BF16), preferred_element_type=F32) + carry_ref[...]
    carry = carry_ref[...] + onehot.sum(axis=0, keepdims=True)
    carry_ref[...] = carry
    cnt_ref[...] = carry

    idx_out = jnp.zeros((tm, LANES), jnp.int32)
    gate_out = jnp.zeros((tm, LANES), F32)
    rank_out = jnp.zeros((tm, LANES), jnp.int32)
    for k in range(TOP_K):
        rank = jnp.sum(jnp.where(lane_e == sels[k], prefix, 0.0), axis=-1, keepdims=True).astype(jnp.int32)
        idx_out = jnp.where(lane_o == k, sels[k], idx_out)
        gate_out = jnp.where(lane_o == k, exps[k] / denom, gate_out)
        rank_out = jnp.where(lane_o == k, rank, rank_out)
    idx_ref[...] = idx_out
    gate_ref[...] = gate_out
    rank_ref[...] = rank_out


def _router(x, mod, rows, tm, w_router, b_router):
    t, d = x.shape
    n_exp = w_router.shape[1]
    return pl.pallas_call(
        _router_kernel,
        grid_spec=pltpu.PrefetchScalarGridSpec(
            num_scalar_prefetch=1, grid=(t // tm,),
            in_specs=[pl.BlockSpec((tm, d), lambda i, r: (i, 0)),
                      pl.BlockSpec((1, 1, d), lambda i, r: (r[i] * 6 + 4, 0, 0)),
                      pl.BlockSpec((1, 1, d), lambda i, r: (r[i] * 6 + 3, 0, 0)),
                      pl.BlockSpec((d, n_exp), lambda i, r: (0, 0)),
                      pl.BlockSpec((1, n_exp), lambda i, r: (0, 0))],
            out_specs=[pl.BlockSpec((tm * d // (2 * LANES), LANES), lambda i, r: (i, 0)),
                       pl.BlockSpec((tm, LANES), lambda i, r: (i, 0)),
                       pl.BlockSpec((tm, LANES), lambda i, r: (i, 0)),
                       pl.BlockSpec((tm, LANES), lambda i, r: (i, 0)),
                       pl.BlockSpec((1, n_exp), lambda i, r: (0, 0))],
            scratch_shapes=[pltpu.VMEM((1, n_exp), F32)]),
        out_shape=[jax.ShapeDtypeStruct((t * d // (2 * LANES), LANES), jnp.uint32),
                   jax.ShapeDtypeStruct((t, LANES), jnp.int32),
                   jax.ShapeDtypeStruct((t, LANES), F32),
                   jax.ShapeDtypeStruct((t, LANES), jnp.int32),
                   jax.ShapeDtypeStruct((1, n_exp), F32)],
        compiler_params=_cparams(("arbitrary",)),
        name="router",
    )(rows, x, mod, mod, w_router, b_router.reshape(1, n_exp))


def _wait_rows(buf_ref, n_rows, sub, sem):
    done = buf_ref.at[pl.ds(0, n_rows * sub), :]
    pltpu.make_async_copy(done, done, sem).wait()


def _gather_kernel(tok_ref, used_ref, src_ref, o_ref, buf0, buf1, sems):
    i = pl.program_id(0)
    last = pl.num_programs(0) - 1
    n_rows = o_ref.shape[0]
    sub = src_ref.shape[1]
    pitch = _row_pitch(sub)
    bufs = (buf0, buf1)

    def issue(block, slot):
        def body(g, carry):
            for j in range(ISSUE_UNROLL):
                r = g * ISSUE_UNROLL + j
                rows = pl.ds(pl.multiple_of(r * pitch, SUBLANES), sub)
                copy = pltpu.make_async_copy(src_ref.at[tok_ref[block * n_rows + r]], bufs[slot].at[rows, :], sems.at[slot])
                copy.start(priority=j % 2)
            return carry
        lax.fori_loop(0, n_rows // ISSUE_UNROLL, body, 0)

    for slot in (0, 1):
        @pl.when(i % 2 == slot)
        def _(slot=slot):
            if slot == 0:
                @pl.when(jnp.logical_and(i == 0, used_ref[0] == 1))
                def _():
                    issue(0, 0)

            @pl.when(jnp.logical_and(i < last, used_ref[jnp.minimum(i + 1, last)] == 1))
            def _():
                issue(i + 1, 1 - slot)

            @pl.when(used_ref[i] == 1)
            def _():
                _wait_rows(bufs[slot], n_rows, sub, sems.at[slot])
                half = sub * LANES
                for s in range(sub):
                    lo, hi = _unpack_bf16_halves(_load_lane_chunk(bufs[slot], 0, n_rows, pitch, s))
                    o_ref[:, s * LANES:(s + 1) * LANES] = lo.astype(o_ref.dtype)
                    o_ref[:, half + s * LANES:half + (s + 1) * LANES] = hi.astype(o_ref.dtype)

    @pl.when(used_ref[i] == 0)
    def _():
        o_ref[...] = jnp.zeros_like(o_ref)


def _gather_rows(slot_tok, block_used, src3, n_slots):
    _, sub, _ = src3.shape
    d = 2 * sub * LANES
    buf = pltpu.VMEM((MOE_BLOCK * _row_pitch(sub), LANES), jnp.uint32)
    return pl.pallas_call(
        _gather_kernel,
        grid_spec=pltpu.PrefetchScalarGridSpec(
            num_scalar_prefetch=2, grid=(n_slots // MOE_BLOCK,),
            in_specs=[pl.BlockSpec(memory_space=pl.ANY)],
            out_specs=pl.BlockSpec((MOE_BLOCK, d), lambda i, t, u: (i, 0)),
            scratch_shapes=[buf, buf, pltpu.SemaphoreType.DMA((2,))]),
        out_shape=jax.ShapeDtypeStruct((n_slots, d), BF16),
        compiler_params=_cparams(("arbitrary",)),
        name="expert_gather",
    )(slot_tok, block_used, src3)


N_SCHED = 8


def _run_item(sched, in_ref, tile_copies, cast_weights, compute, write):
    e_ref, t_ref, flag_ref, ne_ref, nt_ref = sched[0], sched[1], sched[4], sched[6], sched[7]
    w = pl.program_id(0)
    flag = flag_ref[w]
    first_only = sched[5][w] == 1
    half = in_ref.shape[0] // 2

    @pl.when(flag == ITEM_NEW_WEIGHTS)
    def _():
        @pl.when(w == 0)
        def _():
            for c in tile_copies(e_ref[0], t_ref[0]):
                c.start()

        for c in tile_copies(e_ref[w], t_ref[w]):
            c.wait()

    for new_weights in (False, True):
        kind = flag == (ITEM_NEW_WEIGHTS if new_weights else ITEM_COMPUTE)

        @pl.when(jnp.logical_and(kind, jnp.logical_not(first_only)))
        def _(new_weights=new_weights):
            if new_weights:
                cast_weights()
            y = compute(in_ref[...])
            write(0, y[:half])
            write(1, y[half:])

        @pl.when(jnp.logical_and(kind, first_only))
        def _(new_weights=new_weights):
            if new_weights:
                cast_weights()
            write(0, compute(in_ref[:half, :]))
            write(1, None)

    @pl.when(flag == ITEM_FILL)
    def _():
        write(0, None)
        write(1, None)

    @pl.when(jnp.logical_and(flag == ITEM_NEW_WEIGHTS, ne_ref[w] >= 0))
    def _():
        for c in tile_copies(ne_ref[w], nt_ref[w]):
            c.start()


def _expert_up_kernel(*refs, layer, tf, nf):
    sched = refs[:N_SCHED]
    x_ref, w_hbm, bg_ref, bu_ref, o_ref, stage, wg_bf, wu_bf, sems = refs[N_SCHED:]
    half = o_ref.shape[0] // 2

    def tile_copies(e, t):
        cols_g = pl.ds(pl.multiple_of(t * tf, tf), tf)
        cols_u = pl.ds(pl.multiple_of((nf + t) * tf, tf), tf)
        return (pltpu.make_async_copy(w_hbm.at[layer, e, :, cols_g], stage.at[0], sems.at[0]),
                pltpu.make_async_copy(w_hbm.at[layer, e, :, cols_u], stage.at[1], sems.at[1]))

    def cast_weights():
        wg_bf[...] = stage[0].astype(BF16)
        wu_bf[...] = stage[1].astype(BF16)

    def compute(x):
        g = jnp.dot(x, wg_bf[...], preferred_element_type=F32) + bg_ref[0, 0]
        u = jnp.dot(x, wu_bf[...], preferred_element_type=F32) + bu_ref[0, 0]
        g = jnp.minimum(g, SWIGLU_LIMIT)
        u = jnp.clip(u, -SWIGLU_LIMIT, SWIGLU_LIMIT)
        return (g * jax.nn.sigmoid(SWIGLU_ALPHA * g) * (u + 1.0)).astype(BF16)

    def write(which, val):
        rows = pl.ds(which * half, half)
        o_ref[rows, :] = jnp.zeros((half, o_ref.shape[1]), o_ref.dtype) if val is None else val

    _run_item(sched, x_ref, tile_copies, cast_weights, compute, write)


def _expert_up(sched, xs, w_gu, b_gu4, layer, tf):
    n_slots, d = xs.shape
    ff = w_gu.shape[3] // 2
    nf = ff // tf
    n_items = sched[0].shape[0]
    return pl.pallas_call(
        functools.partial(_expert_up_kernel, layer=layer, tf=tf, nf=nf),
        grid_spec=pltpu.PrefetchScalarGridSpec(
            num_scalar_prefetch=N_SCHED, grid=(n_items,),
            in_specs=[pl.BlockSpec((MOE_SUPER, d), lambda w, e, t, b, *_: (b[w], 0)),
                      pl.BlockSpec(memory_space=pl.ANY),
                      pl.BlockSpec((1, 1, 1, tf), lambda w, e, t, *_: (layer, e[w], 0, t[w])),
                      pl.BlockSpec((1, 1, 1, tf), lambda w, e, t, *_: (layer, e[w], 0, nf + t[w]))],
            out_specs=pl.BlockSpec((MOE_SUPER, tf), lambda w, e, t, b, o, *_: (b[w], o[w])),
            scratch_shapes=[pltpu.VMEM((2, d, tf), F32), pltpu.VMEM((d, tf), BF16), pltpu.VMEM((d, tf), BF16),
                            pltpu.SemaphoreType.DMA((2,))]),
        out_shape=jax.ShapeDtypeStruct((n_slots, ff), BF16),
        compiler_params=_cparams(("arbitrary",)),
        name="expert_up",
    )(*sched, xs, w_gu, b_gu4, b_gu4)


def _expert_down_kernel(*refs, layer, tn):
    sched = refs[:N_SCHED]
    h_ref, w_hbm, bd_ref, o_ref, stage, wd_bf, sems = refs[N_SCHED:]
    half = h_ref.shape[0] // 2

    def tile_copies(e, t):
        cols = pl.ds(pl.multiple_of(t * tn, tn), tn)
        return (pltpu.make_async_copy(w_hbm.at[layer, e, :, cols], stage.at[0], sems.at[0]),)

    def cast_weights():
        wd_bf[...] = stage[0].astype(BF16)

    def compute(h):
        return jnp.dot(h, wd_bf[...], preferred_element_type=F32) + bd_ref[0, 0]

    def write(which, val):
        if val is None:
            o_ref[pl.ds(which * half, half), :, :] = jnp.zeros((half,) + o_ref.shape[1:], o_ref.dtype)
        else:
            _store_token_major_3d(o_ref, _pack_bf16_halves(val), which * half)

    _run_item(sched, h_ref, tile_copies, cast_weights, compute, write)


def _expert_down(sched, hs, w_down, b_down4, layer, tn):
    n_slots, ff = hs.shape
    d = w_down.shape[3]
    n_items = sched[0].shape[0]
    return pl.pallas_call(
        functools.partial(_expert_down_kernel, layer=layer, tn=tn),
        grid_spec=pltpu.PrefetchScalarGridSpec(
            num_scalar_prefetch=N_SCHED, grid=(n_items,),
            in_specs=[pl.BlockSpec((MOE_SUPER, ff), lambda w, e, t, b, *_: (b[w], 0)),
                      pl.BlockSpec(memory_space=pl.ANY),
                      pl.BlockSpec((1, 1, 1, tn), lambda w, e, t, *_: (layer, e[w], 0, t[w]))],
            out_specs=pl.BlockSpec((MOE_SUPER, tn // (2 * LANES), LANES), lambda w, e, t, b, o, *_: (b[w], o[w], 0)),
            scratch_shapes=[pltpu.VMEM((1, ff, tn), F32), pltpu.VMEM((ff, tn), BF16),
                            pltpu.SemaphoreType.DMA((1,))]),
        out_shape=jax.ShapeDtypeStruct((n_slots, d // (2 * LANES), LANES), jnp.uint32),
        compiler_params=_cparams(("arbitrary",)),
        name="expert_down",
    )(*sched, hs, w_down, b_down4)


def _expert_schedule(counts, n_blk_e, blk_start, n_tiles, n_blocks):
    i32 = jnp.int32
    n_exp = n_blk_e.shape[0]
    items = n_blk_e * n_tiles
    item_end = jnp.cumsum(items)
    n_used = jnp.sum(n_blk_e)
    total = n_used * n_tiles
    w = jnp.arange(n_tiles * n_blocks, dtype=i32)
    wc = jnp.clip(w, 0, jnp.maximum(total - 1, 0))
    e = jnp.minimum(jnp.sum(item_end[None, :] <= wc[:, None], axis=1), n_exp - 1).astype(i32)
    of_e = e[:, None] == jnp.arange(n_exp, dtype=i32)[None, :]

    def lookup(table):
        return jnp.sum(jnp.where(of_e, table[None, :], 0), axis=1)

    local = wc - lookup(item_end - items)
    nbe = jnp.maximum(lookup(n_blk_e), 1)
    tile = local // nbe
    blk_start = lookup(blk_start)
    valid = w < total
    spare = jnp.maximum(w - total, 0)
    n_unused = jnp.maximum(n_blocks - n_used, 1)
    blk = jnp.where(valid, blk_start + local % nbe, n_used + spare % n_unused)
    out_tile = jnp.where(valid, tile, spare // n_unused)
    flag = jnp.where(valid, jnp.where(local % nbe == 0, ITEM_NEW_WEIGHTS, ITEM_COMPUTE), ITEM_FILL)
    first_only = valid & (lookup(counts) - (local % nbe) * MOE_SUPER <= MOE_SUPER // 2)
    experts = jnp.arange(n_exp, dtype=i32)
    nonempty = n_blk_e > 0
    before = jnp.cumsum(nonempty) - nonempty
    later = nonempty[None, :] & (experts[None, :] > experts[:, None])
    next_nonempty = jnp.min(jnp.where(later, experts[None, :], n_exp), axis=1)
    group = lookup(before) * n_tiles + tile
    last_tile = tile + 1 >= n_tiles
    next_e = jnp.where(last_tile, lookup(next_nonempty), e)
    next_e = jnp.where(group + 1 < jnp.sum(nonempty) * n_tiles, next_e, -1)
    next_tile = jnp.where(last_tile, 0, tile + 1)
    return tuple(a.astype(i32) for a in (e, tile, blk, out_tile, flag, first_only, next_e, next_tile))


def _combine_kernel(dest_ref, rows_ref, y_ref, gate_ref, x_ref, g2_ref, o_ref, buf0, buf1, sems, *, tn):
    del rows_ref
    bufs = (buf0, buf1)
    i = pl.program_id(0)
    tm = x_ref.shape[0]
    sub = y_ref.shape[1]
    pitch = _row_pitch(sub)

    def issue(tile, slot):
        def body(g, carry):
            for u in range(ISSUE_UNROLL):
                p = g * ISSUE_UNROLL + u
                k = p // tm
                t = p - k * tm
                row = dest_ref[(tile * tm + t) * TOP_K + k]
                rows = pl.ds(pl.multiple_of(p * pitch, SUBLANES), sub)
                copy = pltpu.make_async_copy(y_ref.at[row], bufs[slot].at[rows, :], sems.at[slot])
                copy.start(priority=u % 2)
            return carry
        lax.fori_loop(0, TOP_K * tm // ISSUE_UNROLL, body, 0)

    for slot in (0, 1):
        @pl.when(i % 2 == slot)
        def _(slot=slot):
            if slot == 0:
                @pl.when(i == 0)
                def _():
                    issue(0, 0)

            @pl.when(i + 1 < pl.num_programs(0))
            def _():
                issue(i + 1, 1 - slot)

            gate = gate_ref[...]
            buf = bufs[slot]
            _wait_rows(buf, TOP_K * tm, sub, sems.at[slot])
            words_per_tile = tn // (2 * LANES)
            for s in range(sub):
                lo0 = (s // words_per_tile) * tn + (s % words_per_tile) * LANES
                acc_lo = acc_hi = None
                for k in range(TOP_K):
                    lo, hi = _unpack_bf16_halves(_load_lane_chunk(buf, k * tm, tm, pitch, s))
                    g = gate[:, k:k + 1]
                    acc_lo = g * lo if k == 0 else acc_lo + g * lo
                    acc_hi = g * hi if k == 0 else acc_hi + g * hi
                for c0, acc in ((lo0, acc_lo), (lo0 + tn // 2, acc_hi)):
                    cols = slice(c0, c0 + LANES)
                    o_ref[:, cols] = x_ref[:, cols] + g2_ref[0, :, cols] * acc


def _combine(dest_flat, rows, y3, gates, x, mod, tm, tn):
    t, d = x.shape
    sub = y3.shape[1]
    buf = pltpu.VMEM((TOP_K * tm * _row_pitch(sub), LANES), jnp.uint32)
    return pl.pallas_call(
        functools.partial(_combine_kernel, tn=tn),
        grid_spec=pltpu.PrefetchScalarGridSpec(
            num_scalar_prefetch=2, grid=(t // tm,),
            in_specs=[pl.BlockSpec(memory_space=pl.ANY),
                      pl.BlockSpec((tm, LANES), lambda i, dst, r: (i, 0)),
                      pl.BlockSpec((tm, d), lambda i, dst, r: (i, 0)),
                      pl.BlockSpec((1, 1, d), lambda i, dst, r: (r[i] * 6 + 5, 0, 0))],
            out_specs=pl.BlockSpec((tm, d), lambda i, dst, r: (i, 0)),
            scratch_shapes=[buf, buf, pltpu.SemaphoreType.DMA((2,))]),
        out_shape=jax.ShapeDtypeStruct((t, d), F32),
        compiler_params=_cparams(("arbitrary",)),
        name="expert_combine",
    )(dest_flat, rows, y3, gates, x, mod)


def _final_kernel(x_ref, w_ref, o_ref):
    o_ref[...] = _rms(x_ref[...]) * w_ref[...]


def _final_norm(x, w, tm, row0, n_rows):
    d = x.shape[1]
    first = row0 // tm
    return pl.pallas_call(
        _final_kernel,
        grid=(n_rows // tm,),
        in_specs=[pl.BlockSpec((tm, d), lambda i: (first + i, 0)), pl.BlockSpec((1, d), lambda i: (0, 0))],
        out_specs=pl.BlockSpec((tm, d), lambda i: (i, 0)),
        out_shape=jax.ShapeDtypeStruct((n_rows, d), F32),
        compiler_params=_cparams(("parallel",)),
        name="final_norm",
    )(x, w.reshape(1, d))


def _rope_tables(length):
    pairs = HEAD_DIM // 4
    t = np.arange(length)
    row = (t // GRID_W).astype(np.float32)
    col = (t % GRID_W).astype(np.float32)
    inv = jnp.asarray(ROPE_THETA, F32) ** (-jnp.arange(pairs, dtype=F32) / pairs)
    ang = jnp.concatenate([jnp.asarray(row)[:, None] * inv, jnp.asarray(col)[:, None] * inv], axis=-1)
    cos, sin = jnp.cos(ang), jnp.sin(ang)
    return jnp.concatenate([cos, cos], axis=-1), jnp.concatenate([-sin, sin], axis=-1)


def kernel(x_prompt, x_sample, c, cache_na_k, cache_na_v, cache_gqa_k, cache_gqa_v, c_ctx, w_ada, b_ada, w_in, q_norm, k_norm, na_rpb, conv_w, sgu_w, sgu_b, sgu_ln_g, sgu_ln_b, w_branch, w_gate, b_gate, w_out, w_router, b_router, w_gu, b_gu, w_down, b_down, final_norm):
    bp, lp, d = x_prompt.shape
    bs, ls, _ = x_sample.shape
    depth = w_ada.shape[0]
    past = cache_na_k.shape[2]
    tp, ts = bp * lp, bs * ls
    t = tp + ts
    bw = w_branch.shape[2]
    heads = bw // HEAD_DIM
    group = heads // N_KV_HEADS
    n_exp = w_router.shape[2]
    ff = w_down.shape[2]
    n_rows = ls // GRID_W
    assert bw % LANES == 0 and sgu_w.shape[1] * CHUNK == bw and sgu_w.shape[2] == CHUNK

    widths = (bw, bw, bw, heads * HEAD_DIM, N_KV_HEADS * HEAD_DIM, N_KV_HEADS * HEAD_DIM, bw, bw, bw, bw, bw)
    offs = np.concatenate([[0], np.cumsum(widths)])
    blk = [int(o) // LANES for o in offs]
    offs = [int(o) for o in offs]
    assert tp % ls == 0

    x_parts = (x_prompt.reshape(tp, d), x_sample.reshape(ts, d))
    cv = jnp.concatenate([c_ctx[None, :], c, jnp.zeros((8 - 1 - bs, d), F32)], axis=0)
    mod_all = _ada_table(cv, w_ada, b_ada)
    cos_t, sin_t = _rope_tables(ls)
    bias = _na_bias_tables(na_rpb, n_rows)
    cna_k = cache_na_k.reshape(bs, depth, past, heads * HEAD_DIM)
    cna_v = cache_na_v.reshape(bs, depth, past, heads * HEAD_DIM)
    cga_k = cache_gqa_k.reshape(bs, depth, past, N_KV_HEADS * HEAD_DIM)
    cga_v = cache_gqa_v.reshape(bs, depth, past, N_KV_HEADS * HEAD_DIM)

    tm_n = _pick(lp, (256, 128))
    tm_m = _pick(ls, (1024, 512, 256))
    tm_m = tm_m if tp % tm_m == 0 else tm_n
    rows_n = _tile_mod_rows(tm_n, tp, t, ls)
    rows_m = _tile_mod_rows(tm_m, tp, t, ls)
    tm_c = tm_n
    rows_c = _tile_mod_rows(tm_c, tp, t, ls)
    tn = _pick(d, (512, 256, 128))
    tn_in = _pick(w_in.shape[2], (512, 256, 128))
    tf = _pick(ff, (512, 256, 128))
    tn_d = _pick(d, (2048, 1024, 512, 256, 128))
    n_assign = t * TOP_K
    n_blocks = -(-n_assign // MOE_SUPER) + n_exp
    n_slots = n_blocks * MOE_SUPER
    b_gu4 = b_gu.reshape(depth, n_exp, 1, 2 * ff)
    b_down4 = b_down.reshape(depth, n_exp, 1, d)
    w_in_bf, w_gate_bf, w_branch_bf, w_out_bf = (w.astype(BF16) for w in (w_in, w_gate, w_branch, w_out))

    st_na_k, st_na_v, st_ga_k, st_ga_v = [], [], [], []
    for l in range(depth):
        mod = mod_all[l].reshape(8 * 6, 1, d)
        h = _normmod(x_parts, mod, rows_n, tm_n, 1, 0, BF16)
        proj = _matmul(h, w_in_bf, l, F32, tm_m, tn_in)

        y_na_p, na_k_p, na_v_p = _ctx_na_attention(proj, bp, lp, heads, blk[0], blk[1], blk[2])
        y_ga_p, ga_k_p, ga_v_p = _ctx_gqa_attention(proj, q_norm[l], k_norm[l], bp, lp, heads, group,
                                                    blk[3], blk[4], blk[5])
        st_na_k.append(na_k_p.reshape(bp, lp, heads, HEAD_DIM))
        st_na_v.append(na_v_p.reshape(bp, lp, heads, HEAD_DIM))
        st_ga_k.append(ga_k_p.reshape(bp, lp, N_KV_HEADS, HEAD_DIM))
        st_ga_v.append(ga_v_p.reshape(bp, lp, N_KV_HEADS, HEAD_DIM))

        y_na_s = _lat_na_attention(proj, cna_k, cna_v, l, bias, tp, bs, ls, heads, blk[0], blk[1], blk[2])
        y_ga_s = _lat_gqa_attention(proj, cga_k, cga_v, l, cos_t, sin_t, q_norm[l], k_norm[l],
                                    tp, bs, ls, heads, group, blk[3], blk[4], blk[5])

        y_sc = jnp.concatenate([
            _short_conv(proj, conv_w[l], 0, bp, lp, bw, offs[6], offs[7], offs[8]),
            _short_conv(proj, conv_w[l], tp, bs, ls, bw, offs[6], offs[7], offs[8])], axis=0)
        y_sg = _spatial_gating(proj, sgu_ln_g[l], sgu_ln_b[l], sgu_w[l], sgu_b[l], bw, offs[9], offs[10])
        ys = jnp.stack([jnp.concatenate([y_na_p, y_na_s], axis=0), y_sc, y_sg,
                        jnp.concatenate([y_ga_p, y_ga_s], axis=0)], axis=0)

        merged = _gatebranch(h, w_gate_bf, b_gate[l], ys, w_branch_bf, l, tm_m, tn)
        x = _outproj(merged, w_out_bf, l, x_parts, mod, rows_m, tm_m, tn)

        h2, top_idx, gates, rank, counts = _router(x, mod, rows_n, tm_n, w_router[l], b_router[l])
        counts = counts[0].astype(jnp.int32)
        n_blk_e = (counts + MOE_SUPER - 1) // MOE_SUPER
        blk_start = jnp.cumsum(n_blk_e) - n_blk_e
        experts = jnp.arange(n_exp, dtype=jnp.int32)
        of_e = top_idx[:, :TOP_K, None] == experts
        dest = (jnp.sum(jnp.where(of_e, blk_start, 0), axis=-1) * MOE_SUPER + rank[:, :TOP_K]).reshape(-1)
        slot_tok = jnp.zeros((n_slots,), jnp.int32).at[dest].set(jnp.arange(n_assign, dtype=jnp.int32) // TOP_K)
        gblk = jnp.arange(n_slots // MOE_BLOCK, dtype=jnp.int32) * MOE_BLOCK
        in_e = (gblk[:, None] >= blk_start * MOE_SUPER) & (gblk[:, None] < (blk_start + n_blk_e) * MOE_SUPER)
        block_used = jnp.any(in_e & (gblk[:, None] - blk_start * MOE_SUPER < counts), axis=1).astype(jnp.int32)
        xs = _gather_rows(slot_tok, block_used, h2.reshape(t, d // (2 * LANES), LANES), n_slots)
        hs = _expert_up(_expert_schedule(counts, n_blk_e, blk_start, ff // tf, n_blocks), xs, w_gu, b_gu4, l, tf)
        yb = _expert_down(_expert_schedule(counts, n_blk_e, blk_start, d // tn_d, n_blocks), hs, w_down, b_down4,
                          l, tn_d)
        x = _combine(dest, rows_c, yb, gates, x, mod, tm_c, tn_d)
        x_parts = (x,)

    y_prompt = _final_norm(x, final_norm, tm_n, 0, tp).reshape(bp, lp, d)
    y_sample = _final_norm(x, final_norm, tm_n, tp, ts).reshape(bs, ls, d)
    return (y_prompt, y_sample, jnp.stack(st_na_k, axis=1), jnp.stack(st_na_v, axis=1),
            jnp.stack(st_ga_k, axis=1), jnp.stack(st_ga_v, axis=1))
```
